```python
import jax, jax.numpy as jnp
from jax import lax
import numpy as np

D_MODEL = 4096
BATCH = 4
SEQ = 2048
DEPTH = 2
DEC_BATCH = 128
DEC_SEQ = 4
PAST_LEN = 16384
PAGE_SIZE = 128

D_BR = D_MODEL // 2
N_GROUPS = 16
GROUP_DIM = D_BR // N_GROUPS
CONV_A = 3
CONV_C = 31
CHUNK = 128
N_MEM = 256
N_XHEADS = 4
XHEAD_DIM = D_MODEL // N_XHEADS
_FF_RAW = -(-8 * D_MODEL // 3)
D_FF = -(-_FF_RAW // 256) * 256
N_IN = 7 * D_BR + 3 * D_MODEL
EPS = 1e-6

kernel_name = 'hybrid_gated_conv_chunkmlp_conformer_decoder_step'


def _rmsnorm(x, g):
    xf = x.astype(jnp.float32)
    y = xf * lax.rsqrt(jnp.mean(xf * xf, axis=-1, keepdims=True) + EPS)
    return (y * g.astype(jnp.float32)).astype(x.dtype)


def _layernorm(x, g, b):
    xf = x.astype(jnp.float32)
    mu = jnp.mean(xf, axis=-1, keepdims=True)
    var = jnp.mean(jnp.square(xf - mu), axis=-1, keepdims=True)
    y = (xf - mu) * lax.rsqrt(var + EPS) * g.astype(jnp.float32) + b.astype(jnp.float32)
    return y.astype(x.dtype)


def _split_points():
    sizes = (D_BR,) * 7 + (D_MODEL,) * 3
    return np.cumsum(sizes)[:-1].tolist()


def _causal_dwconv(x_ext, w):
    return lax.conv_general_dilated(
        x_ext, w[:, None, :].astype(x_ext.dtype), window_strides=(1,), padding='VALID',
        dimension_numbers=('NWC', 'WIO', 'NWC'), feature_group_count=w.shape[1])


def _spatial_gate(u, v, w_s, b_s):
    bsz, t, _ = v.shape
    L = min(t, CHUNK)
    n = t // L
    mask = jnp.tril(jnp.ones((L, L), dtype=bool))
    ws = jnp.where(mask, w_s[:, :L, :L], 0).astype(v.dtype)
    vg = v.reshape(bsz, n, L, N_GROUPS, GROUP_DIM)
    s = jnp.einsum('gij,bnjgc->bnigc', ws, vg) + b_s[:, :L].T[None, None, :, :, None]
    return u * s.reshape(bsz, t, D_BR)


def _mem_kv(mem, g, w_k, w_v):
    m = _rmsnorm(mem, g)
    bsz = mem.shape[0]
    k = (m @ w_k).reshape(bsz, N_MEM, N_XHEADS, XHEAD_DIM)
    v = (m @ w_v).reshape(bsz, N_MEM, N_XHEADS, XHEAD_DIM)
    return k, v


def _cross_attn(h, k, v, w_q, w_o):
    bsz, t, _ = h.shape
    q = (h @ w_q).reshape(bsz, t, N_XHEADS, XHEAD_DIM)
    s = jnp.einsum('bthd,bmhd->bhtm', q, k).astype(jnp.float32) * (XHEAD_DIM ** -0.5)
    p = jax.nn.softmax(s, axis=-1).astype(v.dtype)
    o = jnp.einsum('bhtm,bmhd->bthd', p, v).reshape(bsz, t, D_MODEL)
    return o @ w_o


def _layer(x, buf_a, buf_c, mk, mv, norm_mix, w_in, b_gate, conv_a_w, ln_b_g, ln_b_b, w_s, b_s,
           conv_c_w, conv_c_b, ln_c_g, ln_c_b, w_branch, w_mix_out, norm_xattn, w_q, w_o,
           norm_ffn, w_gate, w_up, w_down):
    h = _rmsnorm(x, norm_mix)
    z = h @ w_in
    a_h, a_c, a_b, b_u, b_v, c_val, c_gate, g_a, g_b, g_c = jnp.split(z, _split_points(), axis=-1)
    a_ext = jnp.concatenate([buf_a, a_c * a_h], axis=1)
    y_a = (a_b * _causal_dwconv(a_ext, conv_a_w)) @ w_branch[0]
    u = jax.nn.gelu(b_u)
    v = _layernorm(jax.nn.gelu(b_v), ln_b_g, ln_b_b)
    y_b = _spatial_gate(u, v, w_s, b_s) @ w_branch[1]
    c_ext = jnp.concatenate([buf_c, c_val * jax.nn.sigmoid(c_gate)], axis=1)
    c = _causal_dwconv(c_ext, conv_c_w) + conv_c_b
    y_c = jax.nn.silu(_layernorm(c, ln_c_g, ln_c_b)) @ w_branch[2]
    merged = (jax.nn.sigmoid(g_a + b_gate[0]) * y_a
              + jax.nn.sigmoid(g_b + b_gate[1]) * y_b
              + jax.nn.sigmoid(g_c + b_gate[2]) * y_c)
    x = x + merged @ w_mix_out
    x = x + _cross_attn(_rmsnorm(x, norm_xattn), mk, mv, w_q, w_o)
    h = _rmsnorm(x, norm_ffn)
    x = x + (jax.nn.silu(h @ w_gate) * (h @ w_up)) @ w_down
    return x, a_ext[:, -(CONV_A - 1):], c_ext[:, -(CONV_C - 1):], v


def setup_inputs(seed: int = 0) -> dict:
    key = jax.random.key(seed)
    ks = iter(jax.random.split(key, 40))

    def nrm(shape, scale=1.0):
        return jax.random.normal(next(ks), shape, dtype=jnp.float32) * scale

    def gain(shape):
        return 1.0 + nrm(shape, 0.02)

    L = DEPTH
    return {
        'x_prompt': nrm((BATCH, SEQ, D_MODEL)),
        'x_sample': nrm((DEC_BATCH, DEC_SEQ, D_MODEL)),
        'cache_mem_k': nrm((L, DEC_BATCH, N_MEM, N_XHEADS, XHEAD_DIM)),
        'cache_mem_v': nrm((L, DEC_BATCH, N_MEM, N_XHEADS, XHEAD_DIM)),
        'state_conv_a': nrm((L, DEC_BATCH, CONV_A - 1, D_BR)),
        'state_conv_c': nrm((L, DEC_BATCH, CONV_C - 1, D_BR)),
        'mem_prompt': nrm((BATCH, N_MEM, D_MODEL)),
        'norm_mix': gain((L, D_MODEL)),
        'w_in': nrm((L, D_MODEL, N_IN), D_MODEL ** -0.5),
        'b_gate': nrm((L, 3, D_MODEL), 0.01),
        'conv_a_w': nrm((L, CONV_A, D_BR), CONV_A ** -0.5),
        'ln_b_g': gain((L, D_BR)),
        'ln_b_b': nrm((L, D_BR), 0.01),
        'w_s': nrm((L, N_GROUPS, CHUNK, CHUNK), CHUNK ** -0.5),
        'b_s': gain((L, N_GROUPS, CHUNK)),
        'conv_c_w': nrm((L, CONV_C, D_BR), CONV_C ** -0.5),
        'conv_c_b': nrm((L, D_BR), 0.01),
        'ln_c_g': gain((L, D_BR)),
        'ln_c_b': nrm((L, D_BR), 0.01),
        'w_branch': nrm((L, 3, D_BR, D_MODEL), D_BR ** -0.5),
        'w_mix_out': nrm((L, D_MODEL, D_MODEL), D_MODEL ** -0.5),
        'norm_xattn': gain((L, D_MODEL)),
        'norm_mem': gain((L, D_MODEL)),
        'w_q': nrm((L, D_MODEL, D_MODEL), D_MODEL ** -0.5),
        'w_k': nrm((L, D_MODEL, D_MODEL), D_MODEL ** -0.5),
        'w_v': nrm((L, D_MODEL, D_MODEL), D_MODEL ** -0.5),
        'w_o': nrm((L, D_MODEL, D_MODEL), D_MODEL ** -0.5),
        'norm_ffn': gain((L, D_MODEL)),
        'w_gate': nrm((L, D_MODEL, D_FF), D_MODEL ** -0.5),
        'w_up': nrm((L, D_MODEL, D_FF), D_MODEL ** -0.5),
        'w_down': nrm((L, D_FF, D_MODEL), D_FF ** -0.5),
        'norm_final': gain((D_MODEL,)),
    }


def reference(x_prompt, x_sample, cache_mem_k, cache_mem_v, state_conv_a, state_conv_c, mem_prompt,
              norm_mix, w_in, b_gate, conv_a_w, ln_b_g, ln_b_b, w_s, b_s, conv_c_w, conv_c_b,
              ln_c_g, ln_c_b, w_branch, w_mix_out, norm_xattn, norm_mem, w_q, w_k, w_v, w_o,
              norm_ffn, w_gate, w_up, w_down, norm_final):
    yp, ys = x_prompt, x_sample
    nb = x_prompt.shape[0]
    mk_p, mv_p, ca_p, cc_p, ca_s, cc_s, cv_s = [], [], [], [], [], [], []
    for l in range(DEPTH):
        lw = (norm_mix[l], w_in[l], b_gate[l], conv_a_w[l], ln_b_g[l], ln_b_b[l], w_s[l], b_s[l],
              conv_c_w[l], conv_c_b[l], ln_c_g[l], ln_c_b[l], w_branch[l], w_mix_out[l],
              norm_xattn[l], w_q[l], w_o[l], norm_ffn[l], w_gate[l], w_up[l], w_down[l])
        mk, mv = _mem_kv(mem_prompt, norm_mem[l], w_k[l], w_v[l])
        zero_a = jnp.zeros((nb, CONV_A - 1, D_BR), dtype=x_prompt.dtype)
        zero_c = jnp.zeros((nb, CONV_C - 1, D_BR), dtype=x_prompt.dtype)
        yp, ba, bc, _ = _layer(yp, zero_a, zero_c, mk, mv, *lw)
        ys, sa, sc, sv = _layer(ys, state_conv_a[l], state_conv_c[l], cache_mem_k[l], cache_mem_v[l], *lw)
        mk_p.append(mk); mv_p.append(mv); ca_p.append(ba); cc_p.append(bc)
        ca_s.append(sa); cc_s.append(sc); cv_s.append(sv)
    y_prompt = _rmsnorm(yp, norm_final)
    y_sample = _rmsnorm(ys, norm_final)
    return (y_prompt, y_sample, jnp.stack(mk_p), jnp.stack(mv_p), jnp.stack(ca_p), jnp.stack(cc_p),
            jnp.stack(ca_s), jnp.stack(cc_s), jnp.stack(cv_s))
```

```python
import functools
import math

import jax
import jax.numpy as jnp
from jax import lax
from jax.experimental import pallas as pl
from jax.experimental.pallas import tpu as pltpu

EPS = 1e-6
F32 = jnp.float32
BF16 = jnp.bfloat16
V7X_VMEM_BYTES = 64 * 2**20
VMEM_LIMIT_BYTES = V7X_VMEM_BYTES - 8 * 2**20
SUBLANES = 8
LANES = 128
BF16_ROWS = 16


def _tile(n, target, mult):
    best = None
    for t in range(mult, min(n, target) + 1, mult):
        if n % t == 0:
            best = t
    assert best is not None, (n, target, mult)
    return best


def _params(*sem):
    return pltpu.CompilerParams(dimension_semantics=sem, vmem_limit_bytes=VMEM_LIMIT_BYTES)


def _sds(shape, dtype):
    return jax.ShapeDtypeStruct(shape, dtype)


def _rms_body(x_ref, g_ref, o_ref):
    x = x_ref[...]
    y = x * lax.rsqrt(jnp.mean(x * x, axis=-1, keepdims=True) + EPS)
    o_ref[...] = (y * g_ref[...]).astype(o_ref.dtype)


def _rmsnorm(x, g, out_dtype, *, bm, row_block0=0, n_blocks=None):
    m, d = x.shape
    n_blocks = m // bm if n_blocks is None else n_blocks
    return pl.pallas_call(
        _rms_body,
        out_shape=_sds((n_blocks * bm, d), out_dtype),
        grid=(n_blocks,),
        in_specs=[pl.BlockSpec((bm, d), lambda i: (i + row_block0, 0)),
                  pl.BlockSpec((1, d), lambda i: (0, 0))],
        out_specs=pl.BlockSpec((bm, d), lambda i: (i, 0)),
        compiler_params=_params("parallel"),
        name="rmsnorm",
    )(x, g.reshape(1, d))


def _ln_silu_body(c_ref, cb_ref, g_ref, b_ref, o_ref):
    c = c_ref[...] + cb_ref[...]
    mu = jnp.mean(c, axis=-1, keepdims=True)
    cc = c - mu
    var = jnp.mean(cc * cc, axis=-1, keepdims=True)
    y = cc * lax.rsqrt(var + EPS) * g_ref[...] + b_ref[...]
    o_ref[...] = (y * jax.nn.sigmoid(y)).astype(o_ref.dtype)


def _ln_silu(c, conv_bias, g, b, *, bm):
    m, d = c.shape
    vec = pl.BlockSpec((1, d), lambda i: (0, 0))
    return pl.pallas_call(
        _ln_silu_body,
        out_shape=_sds((m, d), BF16),
        grid=(m // bm,),
        in_specs=[pl.BlockSpec((bm, d), lambda i: (i, 0)), vec, vec, vec],
        out_specs=pl.BlockSpec((bm, d), lambda i: (i, 0)),
        compiler_params=_params("parallel"),
        name="ln_silu",
    )(c, conv_bias.reshape(1, d), g.reshape(1, d), b.reshape(1, d))


def _mm_body(x_ref, w_ref, o_ref):
    o_ref[...] = jnp.dot(x_ref[...], w_ref[...], preferred_element_type=F32).astype(o_ref.dtype)


def _matmul(x, w, out_dtype, *, bm, bn):
    m, k = x.shape
    n = w.shape[1]
    return pl.pallas_call(
        _mm_body,
        out_shape=_sds((m, n), out_dtype),
        grid=(n // bn, m // bm),
        in_specs=[pl.BlockSpec((bm, k), lambda j, i: (i, 0)),
                  pl.BlockSpec((k, bn), lambda j, i: (0, j))],
        out_specs=pl.BlockSpec((bm, bn), lambda j, i: (i, j)),
        compiler_params=_params("parallel", "parallel"),
        name="matmul",
    )(x, w)


def _mm_res_body(x_ref, w_ref, r_ref, o_ref):
    o_ref[...] = r_ref[...] + jnp.dot(x_ref[...], w_ref[...], preferred_element_type=F32)


def _matmul_residual(x, w, res, *, bm, bn, weight_stationary=True):
    m, k = x.shape
    n = w.shape[1]
    if weight_stationary:
        grid = (n // bn, m // bm)
        ij = lambda j, i: (i, j)
        xm = lambda j, i: (i, 0)
        wm = lambda j, i: (0, j)
    else:
        grid = (m // bm, n // bn)
        ij = lambda i, j: (i, j)
        xm = lambda i, j: (i, 0)
        wm = lambda i, j: (0, j)
    return pl.pallas_call(
        _mm_res_body,
        out_shape=_sds((m, n), F32),
        grid=grid,
        in_specs=[pl.BlockSpec((bm, k), xm), pl.BlockSpec((k, bn), wm), pl.BlockSpec((bm, bn), ij)],
        out_specs=pl.BlockSpec((bm, bn), ij),
        input_output_aliases={2: 0},
        compiler_params=_params("parallel", "parallel"),
        name="matmul_residual",
    )(x, w, res)


def _branch_body(ya_ref, yb_ref, yc_ref, wa_ref, wb_ref, wc_ref, ga_ref, gb_ref, gc_ref, bg_ref, o_ref):
    acc = None
    for k, (y_ref, w_ref, g_ref) in enumerate(
            ((ya_ref, wa_ref, ga_ref), (yb_ref, wb_ref, gb_ref), (yc_ref, wc_ref, gc_ref))):
        d = jnp.dot(y_ref[...], w_ref[...], preferred_element_type=F32)
        t = jax.nn.sigmoid(g_ref[...] + bg_ref[k:k + 1, :]) * d
        acc = t if acc is None else acc + t
    o_ref[...] = acc.astype(o_ref.dtype)


def _branch_merge(ya, yb, yc, w_branch, z, gate_col0, b_gate, *, bm, bn):
    m, dbr = ya.shape
    n = w_branch.shape[2]
    gblk0 = gate_col0 // bn
    nblk = n // bn
    yspec = pl.BlockSpec((bm, dbr), lambda j, i: (i, 0))

    def wspec(k):
        return pl.BlockSpec((None, dbr, bn), lambda j, i: (k, 0, j))

    def gspec(k):
        return pl.BlockSpec((bm, bn), lambda j, i: (i, gblk0 + k * nblk + j))

    return pl.pallas_call(
        _branch_body,
        out_shape=_sds((m, n), BF16),
        grid=(nblk, m // bm),
        in_specs=[yspec, yspec, yspec, wspec(0), wspec(1), wspec(2), gspec(0), gspec(1), gspec(2),
                  pl.BlockSpec((3, bn), lambda j, i: (0, j))],
        out_specs=pl.BlockSpec((bm, bn), lambda j, i: (i, j)),
        compiler_params=_params("parallel", "parallel"),
        name="branch_merge",
    )(ya, yb, yc, w_branch, w_branch, w_branch, z, z, z, b_gate)


def _swiglu_body(x_ref, wg_ref, wu_ref, o_ref):
    x = x_ref[...]
    g = jnp.dot(x, wg_ref[...], preferred_element_type=F32)
    u = jnp.dot(x, wu_ref[...], preferred_element_type=F32)
    o_ref[...] = (g * jax.nn.sigmoid(g) * u).astype(o_ref.dtype)


def _swiglu_up(x, wg, wu, *, bm, bn):
    m, k = x.shape
    n = wg.shape[1]
    wspec = pl.BlockSpec((k, bn), lambda j, i: (0, j))
    return pl.pallas_call(
        _swiglu_body,
        out_shape=_sds((m, n), BF16),
        grid=(n // bn, m // bm),
        in_specs=[pl.BlockSpec((bm, k), lambda j, i: (i, 0)), wspec, wspec],
        out_specs=pl.BlockSpec((bm, bn), lambda j, i: (i, j)),
        compiler_params=_params("parallel", "parallel"),
        name="swiglu_up",
    )(x, wg, wu)


def _halo_rows(k):
    return -(-(k - 1) // SUBLANES) * SUBLANES


def _conv_prompt_body(*refs, taps, rows, gated):
    if gated:
        h_ref, c_ref, b_ref, w_ref, y_ref, st_ref, ext = refs
        p = c_ref[...] * h_ref[...]
    else:
        h_ref, c_ref, w_ref, y_ref, st_ref, ext = refs
        p = h_ref[...] * jax.nn.sigmoid(c_ref[...])
    halo = _halo_rows(taps)

    @pl.when(pl.program_id(2) == 0)
    def _():
        ext[0:halo, :] = jnp.zeros((halo, ext.shape[1]), F32)

    ext[halo:halo + rows, :] = p
    conv = w_ref[taps - 1:taps, :] * p
    for k in range(taps - 1):
        off = halo - (taps - 1) + k
        conv = conv + w_ref[k:k + 1, :] * ext[off:off + rows, :]
    if gated:
        y_ref[...] = (b_ref[...] * conv).astype(y_ref.dtype)
    else:
        y_ref[...] = conv.astype(y_ref.dtype)
    st_ref[...] = ext[halo + rows - (taps - 1):halo + rows, :]
    ext[0:halo, :] = ext[rows:rows + halo, :]


def _conv_prompt(z, col_blocks, w, out_rows, out_dtype, *, n_seq, seq, rows, cb, gated):
    taps, dbr = w.shape
    nt = seq // rows
    ncb = dbr // cb
    halo = _halo_rows(taps)

    def zspec(col0):
        return pl.BlockSpec((rows, cb), lambda b, c, i: (b * nt + i, col0 // cb + c))

    in_specs = [zspec(c0) for c0 in col_blocks] + [pl.BlockSpec((taps, cb), lambda b, c, i: (0, c))]
    y, st = pl.pallas_call(
        functools.partial(_conv_prompt_body, taps=taps, rows=rows, gated=gated),
        out_shape=(_sds((out_rows, dbr), out_dtype), _sds((n_seq, taps - 1, dbr), F32)),
        grid=(n_seq, ncb, nt),
        in_specs=in_specs,
        out_specs=(pl.BlockSpec((rows, cb), lambda b, c, i: (b * nt + i, c)),
                   pl.BlockSpec((None, taps - 1, cb), lambda b, c, i: (b, 0, c))),
        scratch_shapes=[pltpu.VMEM((halo + rows, cb), F32)],
        compiler_params=_params("arbitrary", "arbitrary", "arbitrary"),
        name="conv_prompt",
    )(*([z] * len(col_blocks)), w)
    return y, st


def _conv_sample_body(*refs, taps, steps, nb, gated):
    if gated:
        h_ref, c_ref, b_ref, st_ref, w_ref, ybuf_ref, y_ref, nst_ref, p_ref = refs
        p_ref[...] = c_ref[...] * h_ref[...]
    else:
        h_ref, c_ref, st_ref, w_ref, ybuf_ref, y_ref, nst_ref, p_ref = refs
        p_ref[...] = h_ref[...] * jax.nn.sigmoid(c_ref[...])
    del ybuf_ref

    def slab(t):
        return slice(t * nb, (t + 1) * nb)

    def ext(e):
        return st_ref[slab(e), :] if e < taps - 1 else p_ref[slab(e - (taps - 1)), :]

    for t in range(steps):
        conv = None
        for k in range(taps):
            term = w_ref[k:k + 1, :] * ext(t + k)
            conv = term if conv is None else conv + term
        if gated:
            conv = b_ref[slab(t), :] * conv
        y_ref[slab(t), :] = conv.astype(y_ref.dtype)
    for j in range(taps - 1):
        nst_ref[slab(j), :] = ext(steps + j)


def _conv_sample(z, col_blocks, w, state_tm, ybuf, *, row0, steps, nb, cb, gated):
    taps, dbr = w.shape
    ncb = dbr // cb
    srows = steps * nb
    rblk = row0 // srows

    def zspec(col0):
        return pl.BlockSpec((srows, cb), lambda c: (rblk, col0 // cb + c))

    in_specs = ([zspec(c0) for c0 in col_blocks]
                + [pl.BlockSpec(((taps - 1) * nb, cb), lambda c: (0, c)),
                   pl.BlockSpec((taps, cb), lambda c: (0, c)),
                   pl.BlockSpec(memory_space=pl.ANY)])
    n_in = len(in_specs)
    return pl.pallas_call(
        functools.partial(_conv_sample_body, taps=taps, steps=steps, nb=nb, gated=gated),
        out_shape=(_sds(ybuf.shape, ybuf.dtype), _sds(((taps - 1) * nb, dbr), F32)),
        grid=(ncb,),
        in_specs=in_specs,
        out_specs=(pl.BlockSpec((srows, cb), lambda c: (rblk, c)),
                   pl.BlockSpec(((taps - 1) * nb, cb), lambda c: (0, c))),
        input_output_aliases={n_in - 1: 0},
        scratch_shapes=[pltpu.VMEM((srows, cb), F32)],
        compiler_params=_params("parallel"),
        name="conv_sample",
    )(*([z] * len(col_blocks)), state_tm, w, ybuf)


def _gelu_ln(x, g, b):
    x = jax.nn.gelu(x)
    mu = jnp.mean(x, axis=-1, keepdims=True)
    xc = x - mu
    var = jnp.mean(xc * xc, axis=-1, keepdims=True)
    return xc * lax.rsqrt(var + EPS) * g + b


def _sgu_prompt_body(u_ref, v_ref, ws_ref, bias_ref, g_ref, b_ref, y_ref, *, groups, gd):
    chunk = u_ref.shape[0]
    v = _gelu_ln(v_ref[...], g_ref[...], b_ref[...])
    row = lax.broadcasted_iota(jnp.int32, (chunk, chunk), 0)
    col = lax.broadcasted_iota(jnp.int32, (chunk, chunk), 1)
    causal = col <= row
    for g in range(groups):
        cols = slice(g * gd, (g + 1) * gd)
        wg = jnp.where(causal, ws_ref[g], 0.0).astype(BF16)
        s = jnp.dot(wg, v[:, cols].astype(BF16), preferred_element_type=F32) + bias_ref[:, cols]
        y_ref[:, cols] = (jax.nn.gelu(u_ref[:, cols]) * s).astype(y_ref.dtype)


def _sgu_prompt(z, u_col0, v_col0, w_s, bias_exp, ln_g, ln_b, out_rows, *, n_rows):
    groups, chunk, _ = w_s.shape
    dbr = bias_exp.shape[1]
    vec = pl.BlockSpec((1, dbr), lambda i: (0, 0))
    return pl.pallas_call(
        functools.partial(_sgu_prompt_body, groups=groups, gd=dbr // groups),
        out_shape=_sds((out_rows, dbr), BF16),
        grid=(n_rows // chunk,),
        in_specs=[pl.BlockSpec((chunk, dbr), lambda i: (i, u_col0 // dbr)),
                  pl.BlockSpec((chunk, dbr), lambda i: (i, v_col0 // dbr)),
                  pl.BlockSpec((groups, chunk, chunk), lambda i: (0, 0, 0)),
                  pl.BlockSpec((chunk, dbr), lambda i: (0, 0)), vec, vec],
        out_specs=pl.BlockSpec((chunk, dbr), lambda i: (i, 0)),
        compiler_params=_params("parallel"),
        name="sgu_prompt",
    )(z, z, w_s, bias_exp, ln_g.reshape(1, dbr), ln_b.reshape(1, dbr))


def _sgu_sample_body(u_ref, v_ref, wexp_ref, bexp_ref, g_ref, b_ref, ybuf_ref, y_ref, vout_ref, *, steps, nb):
    del ybuf_ref
    v = _gelu_ln(v_ref[...], g_ref[...], b_ref[...])
    vout_ref[...] = v
    for i in range(steps):
        s = bexp_ref[i:i + 1, :]
        for j in range(i + 1):
            s = s + wexp_ref[i * steps + j:i * steps + j + 1, :] * v[j * nb:(j + 1) * nb, :]
        rows = slice(i * nb, (i + 1) * nb)
        y_ref[rows, :] = (jax.nn.gelu(u_ref[rows, :]) * s).astype(y_ref.dtype)


def _sgu_sample(z, u_col0, v_col0, wexp, bexp, ln_g, ln_b, ybuf, *, row0, steps, nb):
    dbr = bexp.shape[1]
    srows = steps * nb
    rblk = row0 // srows
    vec = pl.BlockSpec((1, dbr), lambda i: (0, 0))
    return pl.pallas_call(
        functools.partial(_sgu_sample_body, steps=steps, nb=nb),
        out_shape=(_sds(ybuf.shape, ybuf.dtype), _sds((srows, dbr), F32)),
        grid=(1,),
        in_specs=[pl.BlockSpec((srows, dbr), lambda i: (rblk, u_col0 // dbr)),
                  pl.BlockSpec((srows, dbr), lambda i: (rblk, v_col0 // dbr)),
                  pl.BlockSpec((steps * steps, dbr), lambda i: (0, 0)),
                  pl.BlockSpec((steps, dbr), lambda i: (0, 0)), vec, vec,
                  pl.BlockSpec(memory_space=pl.ANY)],
        out_specs=(pl.BlockSpec((srows, dbr), lambda i: (rblk, 0)),
                   pl.BlockSpec((srows, dbr), lambda i: (0, 0))),
        input_output_aliases={6: 0},
        compiler_params=_params("arbitrary"),
        name="sgu_sample",
    )(z, z, wexp, bexp, ln_g.reshape(1, dbr), ln_b.reshape(1, dbr), ybuf)


def _attend(q, k, v, scale):
    s = lax.dot_general(q, k, (((1,), (1,)), ((), ())), preferred_element_type=F32) * scale
    e = jnp.exp(s - jnp.max(s, axis=-1, keepdims=True))
    p = e / jnp.sum(e, axis=-1, keepdims=True)
    return jnp.dot(p.astype(BF16), v, preferred_element_type=F32)


def _attn_prompt_body(q_ref, k_ref, v_ref, o_ref, *, scale):
    o = _attend(q_ref[...], k_ref[...].astype(BF16), v_ref[...].astype(BF16), scale)
    o_ref[...] = o.astype(o_ref.dtype)


def _attn_prompt(q, k, v, *, n_seq, seq, n_heads, tq):
    rows, d = q.shape
    hd = d // n_heads
    n_mem = k.shape[0] // n_seq
    nt = seq // tq
    kv = pl.BlockSpec((n_mem, hd), lambda b, h, i: (b, h))
    qo = pl.BlockSpec((tq, hd), lambda b, h, i: (b * nt + i, h))
    return pl.pallas_call(
        functools.partial(_attn_prompt_body, scale=hd ** -0.5),
        out_shape=_sds((rows, d), BF16),
        grid=(n_seq, n_heads, nt),
        in_specs=[qo, kv, kv],
        out_specs=qo,
        compiler_params=_params("parallel", "parallel", "parallel"),
        name="attn_prompt",
    )(q, k, v)


def _attn_sample_body(q_ref, k_ref, v_ref, o_ref, *, scale, n_heads):
    hd = q_ref.shape[-1] // n_heads
    for b in range(q_ref.shape[0]):
        for h in range(n_heads):
            cols = slice(h * hd, (h + 1) * hd)
            o = _attend(q_ref[b, :, cols], k_ref[b, :, cols].astype(BF16), v_ref[b, :, cols].astype(BF16), scale)
            o_ref[b, :, cols] = o.astype(o_ref.dtype)


def _attn_sample(q, k, v, *, n_heads, bb):
    nb, tpad, d = q.shape
    n_mem = k.shape[1]
    qo = pl.BlockSpec((bb, tpad, d), lambda i: (i, 0, 0))
    kv = pl.BlockSpec((bb, n_mem, d), lambda i: (i, 0, 0))
    return pl.pallas_call(
        functools.partial(_attn_sample_body, scale=(d // n_heads) ** -0.5, n_heads=n_heads),
        out_shape=_sds((nb, tpad, d), BF16),
        grid=(nb // bb,),
        in_specs=[qo, kv, kv],
        out_specs=qo,
        compiler_params=_params("parallel"),
        name="attn_sample",
    )(q, k, v)


def kernel(x_prompt, x_sample, cache_mem_k, cache_mem_v, state_conv_a, state_conv_c, mem_prompt, norm_mix, w_in, b_gate, conv_a_w, ln_b_g, ln_b_b, w_s, b_s, conv_c_w, conv_c_b, ln_c_g, ln_c_b, w_branch, w_mix_out, norm_xattn, norm_mem, w_q, w_k, w_v, w_o, norm_ffn, w_gate, w_up, w_down, norm_final):
    n_seq, seq, d = x_prompt.shape
    nb, steps, _ = x_sample.shape
    depth = w_in.shape[0]
    dbr = conv_a_w.shape[-1]
    taps_a, taps_c = conv_a_w.shape[1], conv_c_w.shape[1]
    groups, chunk = w_s.shape[1], w_s.shape[2]
    gd = dbr // groups
    n_mem, n_heads, hd = cache_mem_k.shape[2:]
    mp = n_seq * seq
    ms = nb * steps
    m = mp + ms
    assert steps <= chunk and seq % chunk == 0 and mp % ms == 0 and n_heads * hd == d
    col = {name: i * dbr for i, name in enumerate(("a_h", "a_c", "a_b", "b_u", "b_v", "c_val", "c_gate"))}
    gate_col0 = 7 * dbr

    dff = w_gate.shape[-1]
    bm = _tile(m, 1088, BF16_ROWS)
    bm_half = _tile(m, 544, BF16_ROWS)
    bm_norm = _tile(math.gcd(mp, ms), 256, BF16_ROWS)
    bn_wide = _tile(d, 1024, LANES)
    bn_res = _tile(d, 512, LANES)
    bn_in = _tile(w_in.shape[-1], 1024, LANES)
    bn_gate = _tile(math.gcd(d, gate_col0), 512, LANES)
    bn_ff = _tile(dff, 256, LANES)
    bn_down = _tile(d, 256, LANES)
    rows_conv = _tile(seq, 256, SUBLANES)
    cb_a = _tile(dbr, 512, LANES)
    cb_c = _tile(dbr, 256, LANES)
    tq = _tile(seq, 512, BF16_ROWS)

    def tm(a):
        return jnp.swapaxes(a, 0, 1).reshape(a.shape[1] * nb, a.shape[2])

    def bmaj(a, t):
        return jnp.swapaxes(a.reshape(t, nb, a.shape[1]), 0, 1)

    x = jnp.concatenate([x_prompt.reshape(mp, d), tm(x_sample)], axis=0)
    mem = mem_prompt.reshape(n_seq * n_mem, d)
    tpad = -(-steps // BF16_ROWS) * BF16_ROWS

    outs = {k: [] for k in ("mk", "mv", "ca_p", "cc_p", "ca_s", "cc_s", "cv_s")}
    for l in range(depth):
        h = _rmsnorm(x, norm_mix[l], BF16, bm=bm_norm)
        z = _matmul(h, w_in[l].astype(BF16), F32, bm=bm, bn=bn_in)

        ya, ca_p = _conv_prompt(z, (col["a_h"], col["a_c"], col["a_b"]), conv_a_w[l], m, BF16,
                                n_seq=n_seq, seq=seq, rows=rows_conv, cb=cb_a, gated=True)
        ya, ca_s = _conv_sample(z, (col["a_h"], col["a_c"], col["a_b"]), conv_a_w[l], tm(state_conv_a[l]), ya,
                                row0=mp, steps=steps, nb=nb, cb=cb_a, gated=True)

        bias_exp = jnp.repeat(b_s[l].T, gd, axis=1)
        wexp = jnp.repeat(w_s[l][:, :steps, :steps].transpose(1, 2, 0).reshape(steps * steps, groups), gd, axis=1)
        yb = _sgu_prompt(z, col["b_u"], col["b_v"], w_s[l], bias_exp, ln_b_g[l], ln_b_b[l], m, n_rows=mp)
        yb, cv_s = _sgu_sample(z, col["b_u"], col["b_v"], wexp, bias_exp[:steps], ln_b_g[l], ln_b_b[l], yb,
                               row0=mp, steps=steps, nb=nb)

        cc, cc_p = _conv_prompt(z, (col["c_val"], col["c_gate"]), conv_c_w[l], m, F32,
                                n_seq=n_seq, seq=seq, rows=rows_conv, cb=cb_c, gated=False)
        cc, cc_s = _conv_sample(z, (col["c_val"], col["c_gate"]), conv_c_w[l], tm(state_conv_c[l]), cc,
                                row0=mp, steps=steps, nb=nb, cb=cb_c, gated=False)
        yc = _ln_silu(cc, conv_c_b[l], ln_c_g[l], ln_c_b[l], bm=bm_norm)

        merged = _branch_merge(ya, yb, yc, w_branch[l].astype(BF16), z, gate_col0, b_gate[l], bm=bm_half, bn=bn_gate)
        x = _matmul_residual(merged, w_mix_out[l].astype(BF16), x, bm=bm, bn=bn_res)

        hm = _rmsnorm(mem, norm_mem[l], BF16, bm=_tile(mem.shape[0], 256, BF16_ROWS))
        mk = _matmul(hm, w_k[l].astype(BF16), F32, bm=_tile(mem.shape[0], 1024, BF16_ROWS), bn=bn_wide)
        mv = _matmul(hm, w_v[l].astype(BF16), F32, bm=_tile(mem.shape[0], 1024, BF16_ROWS), bn=bn_wide)
        hq = _rmsnorm(x, norm_xattn[l], BF16, bm=bm_norm)
        q = _matmul(hq, w_q[l].astype(BF16), BF16, bm=bm, bn=bn_wide)
        o = _attn_prompt(q, mk, mv, n_seq=n_seq, seq=seq, n_heads=n_heads, tq=tq)
        q_s = jnp.pad(bmaj(q[mp:], steps), ((0, 0), (0, tpad - steps), (0, 0)))
        o_s = _attn_sample(q_s, cache_mem_k[l].reshape(nb, n_mem, d), cache_mem_v[l].reshape(nb, n_mem, d),
                           n_heads=n_heads, bb=1)
        o = lax.dynamic_update_slice(o, tm(o_s[:, :steps]), (mp, 0))
        x = _matmul_residual(o, w_o[l].astype(BF16), x, bm=bm, bn=bn_res)

        hf = _rmsnorm(x, norm_ffn[l], BF16, bm=bm_norm)
        act = _swiglu_up(hf, w_gate[l].astype(BF16), w_up[l].astype(BF16), bm=bm, bn=bn_ff)
        x = _matmul_residual(act, w_down[l].astype(BF16), x, bm=bm_half, bn=bn_down, weight_stationary=False)

        outs["mk"].append(mk.reshape(n_seq, n_mem, n_heads, hd))
        outs["mv"].append(mv.reshape(n_seq, n_mem, n_heads, hd))
        outs["ca_p"].append(ca_p)
        outs["cc_p"].append(cc_p)
        outs["ca_s"].append(bmaj(ca_s, taps_a - 1))
        outs["cc_s"].append(bmaj(cc_s, taps_c - 1))
        outs["cv_s"].append(bmaj(cv_s, steps))

    y_prompt = _rmsnorm(x, norm_final, F32, bm=bm_norm, n_blocks=mp // bm_norm).reshape(n_seq, seq, d)
    y_sample = bmaj(_rmsnorm(x, norm_final, F32, bm=bm_norm, row_block0=mp // bm_norm, n_blocks=ms // bm_norm), steps)
    return (y_prompt, y_sample, jnp.stack(outs["mk"]), jnp.stack(outs["mv"]), jnp.stack(outs["ca_p"]),
            jnp.stack(outs["cc_p"]), jnp.stack(outs["ca_s"]), jnp.stack(outs["cc_s"]), jnp.stack(outs["cv_s"]))
```

```python
import functools
import math

import jax
import jax.numpy as jnp
from jax import lax
from jax.experimental import pallas as pl
from jax.experimental.pallas import tpu as pltpu

EPS = 1e-6
F32 = jnp.float32
BF16 = jnp.bfloat16
V7X_VMEM_BYTES = 64 * 2**20
VMEM_LIMIT_BYTES = V7X_VMEM_BYTES - 8 * 2**20
SUBLANES = 8
LANES = 128
BF16_ROWS = 16
ACC_VREGS = 16


def _tile(n, target, mult):
    best = None
    for t in range(mult, min(n, target) + 1, mult):
        if n % t == 0:
            best = t
    assert best is not None, (n, target, mult)
    return best


def _params(*sem):
    return pltpu.CompilerParams(dimension_semantics=sem, vmem_limit_bytes=VMEM_LIMIT_BYTES)


def _sds(shape, dtype):
    return jax.ShapeDtypeStruct(shape, dtype)


def _rms_body(x_ref, g_ref, o_ref):
    x = x_ref[...]
    y = x * lax.rsqrt(jnp.mean(x * x, axis=-1, keepdims=True) + EPS)
    o_ref[...] = (y * g_ref[...]).astype(o_ref.dtype)


def _rmsnorm(x, g, out_dtype, *, bm, row_block0=0, n_blocks=None):
    m, d = x.shape
    n_blocks = m // bm if n_blocks is None else n_blocks
    return pl.pallas_call(
        _rms_body,
        out_shape=_sds((n_blocks * bm, d), out_dtype),
        grid=(n_blocks,),
        in_specs=[pl.BlockSpec((bm, d), lambda i: (i + row_block0, 0)),
                  pl.BlockSpec((1, d), lambda i: (0, 0))],
        out_specs=pl.BlockSpec((bm, d), lambda i: (i, 0)),
        compiler_params=_params("parallel"),
        name="rmsnorm",
    )(x, g.reshape(1, d))


def _ln_silu_body(c_ref, cb_ref, g_ref, b_ref, o_ref):
    c = c_ref[...] + cb_ref[...]
    mu = jnp.mean(c, axis=-1, keepdims=True)
    cc = c - mu
    var = jnp.mean(cc * cc, axis=-1, keepdims=True)
    y = cc * lax.rsqrt(var + EPS) * g_ref[...] + b_ref[...]
    o_ref[...] = (y * jax.nn.sigmoid(y)).astype(o_ref.dtype)


def _ln_silu(c, conv_bias, g, b, *, bm):
    m, d = c.shape
    vec = pl.BlockSpec((1, d), lambda i: (0, 0))
    return pl.pallas_call(
        _ln_silu_body,
        out_shape=_sds((m, d), BF16),
        grid=(m // bm,),
        in_specs=[pl.BlockSpec((bm, d), lambda i: (i, 0)), vec, vec, vec],
        out_specs=pl.BlockSpec((bm, d), lambda i: (i, 0)),
        compiler_params=_params("parallel"),
        name="ln_silu",
    )(c, conv_bias.reshape(1, d), g.reshape(1, d), b.reshape(1, d))


def _cast_body(w_ref, o_ref):
    o_ref[...] = w_ref[...].astype(o_ref.dtype)


def _cast_bf16(w):
    n = w.shape[-1]
    rows = math.prod(w.shape[:-1])
    br = _tile(rows, 512, BF16_ROWS)
    out = pl.pallas_call(
        _cast_body,
        out_shape=_sds((rows, n), BF16),
        grid=(rows // br,),
        in_specs=[pl.BlockSpec((br, n), lambda i: (i, 0))],
        out_specs=pl.BlockSpec((br, n), lambda i: (i, 0)),
        compiler_params=_params("parallel"),
        name="cast_bf16",
    )(w.reshape(rows, n))
    return out.reshape(w.shape)


def _round_weight(w_ref, wbf_ref):
    @pl.when(pl.program_id(1) == 0)
    def _():
        wbf_ref[...] = w_ref[...].astype(BF16)


def _mm_body(x_ref, w_ref, o_ref, wbf_ref):
    _round_weight(w_ref, wbf_ref)
    o_ref[...] = jnp.dot(x_ref[...], wbf_ref[...], preferred_element_type=F32).astype(o_ref.dtype)


def _matmul(x, w, layer, out_dtype, *, bm, bn):
    m, k = x.shape
    n = w.shape[2]
    return pl.pallas_call(
        _mm_body,
        out_shape=_sds((m, n), out_dtype),
        grid=(n // bn, m // bm),
        in_specs=[pl.BlockSpec((bm, k), lambda j, i: (i, 0)),
                  pl.BlockSpec((None, k, bn), lambda j, i: (layer, 0, j))],
        out_specs=pl.BlockSpec((bm, bn), lambda j, i: (i, j)),
        scratch_shapes=[pltpu.VMEM((k, bn), BF16)],
        compiler_params=_params("arbitrary", "arbitrary"),
        name="matmul",
    )(x, w)


def _mm_res_body(x_ref, w_ref, r_ref, o_ref, wbf_ref):
    _round_weight(w_ref, wbf_ref)
    o_ref[...] = r_ref[...] + jnp.dot(x_ref[...], wbf_ref[...], preferred_element_type=F32)


def _matmul_residual(x, w, layer, res, *, bm, bn):
    m, k = x.shape
    n = w.shape[2]
    ij = lambda j, i: (i, j)
    return pl.pallas_call(
        _mm_res_body,
        out_shape=_sds((m, n), F32),
        grid=(n // bn, m // bm),
        in_specs=[pl.BlockSpec((bm, k), lambda j, i: (i, 0)),
                  pl.BlockSpec((None, k, bn), lambda j, i: (layer, 0, j)),
                  pl.BlockSpec((bm, bn), ij)],
        out_specs=pl.BlockSpec((bm, bn), ij),
        scratch_shapes=[pltpu.VMEM((k, bn), BF16)],
        input_output_aliases={2: 0},
        compiler_params=_params("arbitrary", "arbitrary"),
        name="matmul_residual",
    )(x, w, res)


def _mm_res_rows_body(x_ref, w_ref, r_ref, o_ref):
    o_ref[...] = r_ref[...] + jnp.dot(x_ref[...], w_ref[...], preferred_element_type=F32)


def _matmul_residual_rows(x, w, layer, res, *, bm, bn):
    m, k = x.shape
    n = w.shape[2]
    ij = lambda i, j: (i, j)
    return pl.pallas_call(
        _mm_res_rows_body,
        out_shape=_sds((m, n), F32),
        grid=(m // bm, n // bn),
        in_specs=[pl.BlockSpec((bm, k), lambda i, j: (i, 0)),
                  pl.BlockSpec((None, k, bn), lambda i, j: (layer, 0, j)),
                  pl.BlockSpec((bm, bn), ij)],
        out_specs=pl.BlockSpec((bm, bn), ij),
        input_output_aliases={2: 0},
        compiler_params=_params("parallel", "parallel"),
        name="matmul_residual_rows",
    )(x, w, res)


def _branch_body(ya_ref, yb_ref, yc_ref, wa_ref, wb_ref, wc_ref, ga_ref, gb_ref, gc_ref, bg_ref, o_ref):
    acc = None
    for k, (y_ref, w_ref, g_ref) in enumerate(
            ((ya_ref, wa_ref, ga_ref), (yb_ref, wb_ref, gb_ref), (yc_ref, wc_ref, gc_ref))):
        d = jnp.dot(y_ref[...], w_ref[...], preferred_element_type=F32)
        t = jax.nn.sigmoid(g_ref[...] + bg_ref[k:k + 1, :]) * d
        acc = t if acc is None else acc + t
    o_ref[...] = acc.astype(o_ref.dtype)


def _branch_merge(ya, yb, yc, w_branch, layer, z, gate_col0, b_gate, *, bm, bn):
    m, dbr = ya.shape
    n = w_branch.shape[3]
    gblk0 = gate_col0 // bn
    nblk = n // bn
    yspec = pl.BlockSpec((bm, dbr), lambda j, i: (i, 0))

    def wspec(k):
        return pl.BlockSpec((None, None, dbr, bn), lambda j, i: (layer, k, 0, j))

    def gspec(k):
        return pl.BlockSpec((bm, bn), lambda j, i: (i, gblk0 + k * nblk + j))

    return pl.pallas_call(
        _branch_body,
        out_shape=_sds((m, n), BF16),
        grid=(nblk, m // bm),
        in_specs=[yspec, yspec, yspec, wspec(0), wspec(1), wspec(2), gspec(0), gspec(1), gspec(2),
                  pl.BlockSpec((3, bn), lambda j, i: (0, j))],
        out_specs=pl.BlockSpec((bm, bn), lambda j, i: (i, j)),
        compiler_params=_params("parallel", "parallel"),
        name="branch_merge",
    )(ya, yb, yc, w_branch, w_branch, w_branch, z, z, z, b_gate)


def _swiglu_body(x_ref, wg_ref, wu_ref, o_ref, wgbf_ref, wubf_ref):
    _round_weight(wg_ref, wgbf_ref)
    _round_weight(wu_ref, wubf_ref)
    x = x_ref[...]
    g = jnp.dot(x, wgbf_ref[...], preferred_element_type=F32)
    u = jnp.dot(x, wubf_ref[...], preferred_element_type=F32)
    o_ref[...] = (g * jax.nn.sigmoid(g) * u).astype(o_ref.dtype)


def _swiglu_up(x, wg, wu, layer, *, bm, bn):
    m, k = x.shape
    n = wg.shape[2]
    wspec = pl.BlockSpec((None, k, bn), lambda j, i: (layer, 0, j))
    return pl.pallas_call(
        _swiglu_body,
        out_shape=_sds((m, n), BF16),
        grid=(n // bn, m // bm),
        in_specs=[pl.BlockSpec((bm, k), lambda j, i: (i, 0)), wspec, wspec],
        out_specs=pl.BlockSpec((bm, bn), lambda j, i: (i, j)),
        scratch_shapes=[pltpu.VMEM((k, bn), BF16), pltpu.VMEM((k, bn), BF16)],
        compiler_params=_params("arbitrary", "arbitrary"),
        name="swiglu_up",
    )(x, wg, wu)


def _halo_rows(k):
    return -(-(k - 1) // SUBLANES) * SUBLANES


def _conv_prompt_body(*refs, taps, rows, gated, shifted):
    if gated:
        h_ref, c_ref, b_ref, w_ref, y_ref, st_ref, ext = refs[:7]
        p = c_ref[...] * h_ref[...]
    else:
        h_ref, c_ref, w_ref, y_ref, st_ref, ext = refs[:6]
        p = h_ref[...] * jax.nn.sigmoid(c_ref[...])
    ph = refs[-1] if shifted else None
    halo = _halo_rows(taps)
    cb = ext.shape[1]

    @pl.when(pl.program_id(2) == 0)
    def _():
        ext[0:halo, :] = jnp.zeros((halo, cb), F32)

    ext[halo:halo + rows, :] = p
    if shifted:
        n = halo + rows - SUBLANES
        for s in range(1, SUBLANES):
            ph[s - 1, 0:n, :] = ext[s:s + n, :]

    def window(off, r0, nr):
        a, s = divmod(off, SUBLANES)
        if s == 0 or not shifted:
            return ext[off + r0:off + r0 + nr, :]
        return ph[s - 1, a * SUBLANES + r0:a * SUBLANES + r0 + nr, :]

    nr = max(SUBLANES, ACC_VREGS * SUBLANES * LANES // cb)
    for r0 in range(0, rows, nr):
        conv = None
        for k in range(taps):
            term = w_ref[k:k + 1, :] * window(halo - (taps - 1) + k, r0, nr)
            conv = term if conv is None else conv + term
        if gated:
            conv = b_ref[r0:r0 + nr, :] * conv
        y_ref[r0:r0 + nr, :] = conv.astype(y_ref.dtype)
    st_ref[...] = ext[halo + rows - (taps - 1):halo + rows, :]
    ext[0:halo, :] = ext[rows:rows + halo, :]


def _conv_prompt(z, col_blocks, w, out_rows, out_dtype, *, n_seq, seq, rows, cb, gated):
    taps, dbr = w.shape
    nt = seq // rows
    ncb = dbr // cb
    halo = _halo_rows(taps)
    shifted = taps - 1 > SUBLANES
    assert rows % max(SUBLANES, ACC_VREGS * SUBLANES * LANES // cb) == 0

    def zspec(col0):
        return pl.BlockSpec((rows, cb), lambda b, c, i: (b * nt + i, col0 // cb + c))

    in_specs = [zspec(c0) for c0 in col_blocks] + [pl.BlockSpec((taps, cb), lambda b, c, i: (0, c))]
    scratch = [pltpu.VMEM((halo + rows, cb), F32)]
    if shifted:
        scratch.append(pltpu.VMEM((SUBLANES - 1, halo + rows, cb), F32))
    y, st = pl.pallas_call(
        functools.partial(_conv_prompt_body, taps=taps, rows=rows, gated=gated, shifted=shifted),
        out_shape=(_sds((out_rows, dbr), out_dtype), _sds((n_seq, taps - 1, dbr), F32)),
        grid=(n_seq, ncb, nt),
        in_specs=in_specs,
        out_specs=(pl.BlockSpec((rows, cb), lambda b, c, i: (b * nt + i, c)),
                   pl.BlockSpec((None, taps - 1, cb), lambda b, c, i: (b, 0, c))),
        scratch_shapes=scratch,
        compiler_params=_params("arbitrary", "arbitrary", "arbitrary"),
        name="conv_prompt",
    )(*([z] * len(col_blocks)), w)
    return y, st


def _conv_sample_body(*refs, taps, steps, nb, gated):
    if gated:
        h_ref, c_ref, b_ref, st_ref, w_ref, ybuf_ref, y_ref, nst_ref, p_ref = refs
        p_ref[...] = c_ref[...] * h_ref[...]
    else:
        h_ref, c_ref, st_ref, w_ref, ybuf_ref, y_ref, nst_ref, p_ref = refs
        p_ref[...] = h_ref[...] * jax.nn.sigmoid(c_ref[...])
    del ybuf_ref

    def slab(t):
        return slice(t * nb, (t + 1) * nb)

    def ext(e):
        return st_ref[slab(e), :] if e < taps - 1 else p_ref[slab(e - (taps - 1)), :]

    for t in range(steps):
        conv = None
        for k in range(taps):
            term = w_ref[k:k + 1, :] * ext(t + k)
            conv = term if conv is None else conv + term
        if gated:
            conv = b_ref[slab(t), :] * conv
        y_ref[slab(t), :] = conv.astype(y_ref.dtype)
    for j in range(taps - 1):
        nst_ref[slab(j), :] = ext(steps + j)


def _conv_sample(z, col_blocks, w, state_tm, ybuf, *, row0, steps, nb, cb, gated):
    taps, dbr = w.shape
    ncb = dbr // cb
    srows = steps * nb
    rblk = row0 // srows

    def zspec(col0):
        return pl.BlockSpec((srows, cb), lambda c: (rblk, col0 // cb + c))

    in_specs = ([zspec(c0) for c0 in col_blocks]
                + [pl.BlockSpec(((taps - 1) * nb, cb), lambda c: (0, c)),
                   pl.BlockSpec((taps, cb), lambda c: (0, c)),
                   pl.BlockSpec(memory_space=pl.ANY)])
    n_in = len(in_specs)
    return pl.pallas_call(
        functools.partial(_conv_sample_body, taps=taps, steps=steps, nb=nb, gated=gated),
        out_shape=(_sds(ybuf.shape, ybuf.dtype), _sds(((taps - 1) * nb, dbr), F32)),
        grid=(ncb,),
        in_specs=in_specs,
        out_specs=(pl.BlockSpec((srows, cb), lambda c: (rblk, c)),
                   pl.BlockSpec(((taps - 1) * nb, cb), lambda c: (0, c))),
        input_output_aliases={n_in - 1: 0},
        scratch_shapes=[pltpu.VMEM((srows, cb), F32)],
        compiler_params=_params("parallel"),
        name="conv_sample",
    )(*([z] * len(col_blocks)), state_tm, w, ybuf)


def _gelu_ln(x, g, b):
    x = jax.nn.gelu(x)
    mu = jnp.mean(x, axis=-1, keepdims=True)
    xc = x - mu
    var = jnp.mean(xc * xc, axis=-1, keepdims=True)
    return xc * lax.rsqrt(var + EPS) * g + b


def _sgu_prompt_body(u_ref, v_ref, ws_ref, bias_ref, g_ref, b_ref, y_ref, *, groups, gd):
    chunk = u_ref.shape[0]
    v = _gelu_ln(v_ref[...], g_ref[...], b_ref[...])
    row = lax.broadcasted_iota(jnp.int32, (chunk, chunk), 0)
    col = lax.broadcasted_iota(jnp.int32, (chunk, chunk), 1)
    causal = col <= row
    for g in range(groups):
        cols = slice(g * gd, (g + 1) * gd)
        wg = jnp.where(causal, ws_ref[g], 0.0).astype(BF16)
        s = jnp.dot(wg, v[:, cols].astype(BF16), preferred_element_type=F32) + bias_ref[:, cols]
        y_ref[:, cols] = (jax.nn.gelu(u_ref[:, cols]) * s).astype(y_ref.dtype)


def _sgu_prompt(z, u_col0, v_col0, w_s, bias_exp, ln_g, ln_b, out_rows, *, n_rows):
    groups, chunk, _ = w_s.shape
    dbr = bias_exp.shape[1]
    vec = pl.BlockSpec((1, dbr), lambda i: (0, 0))
    return pl.pallas_call(
        functools.partial(_sgu_prompt_body, groups=groups, gd=dbr // groups),
        out_shape=_sds((out_rows, dbr), BF16),
        grid=(n_rows // chunk,),
        in_specs=[pl.BlockSpec((chunk, dbr), lambda i: (i, u_col0 // dbr)),
                  pl.BlockSpec((chunk, dbr), lambda i: (i, v_col0 // dbr)),
                  pl.BlockSpec((groups, chunk, chunk), lambda i: (0, 0, 0)),
                  pl.BlockSpec((chunk, dbr), lambda i: (0, 0)), vec, vec],
        out_specs=pl.BlockSpec((chunk, dbr), lambda i: (i, 0)),
        compiler_params=_params("parallel"),
        name="sgu_prompt",
    )(z, z, w_s, bias_exp, ln_g.reshape(1, dbr), ln_b.reshape(1, dbr))


def _sgu_sample_body(u_ref, v_ref, wexp_ref, bexp_ref, g_ref, b_ref, ybuf_ref, y_ref, vout_ref, *, steps, nb):
    del ybuf_ref
    v = _gelu_ln(v_ref[...], g_ref[...], b_ref[...])
    vout_ref[...] = v
    for i in range(steps):
        s = bexp_ref[i:i + 1, :]
        for j in range(i + 1):
            s = s + wexp_ref[i * steps + j:i * steps + j + 1, :] * v[j * nb:(j + 1) * nb, :]
        rows = slice(i * nb, (i + 1) * nb)
        y_ref[rows, :] = (jax.nn.gelu(u_ref[rows, :]) * s).astype(y_ref.dtype)


def _sgu_sample(z, u_col0, v_col0, wexp, bexp, ln_g, ln_b, ybuf, *, row0, steps, nb):
    dbr = bexp.shape[1]
    srows = steps * nb
    rblk = row0 // srows
    vec = pl.BlockSpec((1, dbr), lambda i: (0, 0))
    return pl.pallas_call(
        functools.partial(_sgu_sample_body, steps=steps, nb=nb),
        out_shape=(_sds(ybuf.shape, ybuf.dtype), _sds((srows, dbr), F32)),
        grid=(1,),
        in_specs=[pl.BlockSpec((srows, dbr), lambda i: (rblk, u_col0 // dbr)),
                  pl.BlockSpec((srows, dbr), lambda i: (rblk, v_col0 // dbr)),
                  pl.BlockSpec((steps * steps, dbr), lambda i: (0, 0)),
                  pl.BlockSpec((steps, dbr), lambda i: (0, 0)), vec, vec,
                  pl.BlockSpec(memory_space=pl.ANY)],
        out_specs=(pl.BlockSpec((srows, dbr), lambda i: (rblk, 0)),
                   pl.BlockSpec((srows, dbr), lambda i: (0, 0))),
        input_output_aliases={6: 0},
        compiler_params=_params("arbitrary"),
        name="sgu_sample",
    )(z, z, wexp, bexp, ln_g.reshape(1, dbr), ln_b.reshape(1, dbr), ybuf)


def _attend(q, k, v, scale):
    s = lax.dot_general(q, k, (((1,), (1,)), ((), ())), preferred_element_type=F32) * scale
    e = jnp.exp(s - jnp.max(s, axis=-1, keepdims=True))
    p = e / jnp.sum(e, axis=-1, keepdims=True)
    return jnp.dot(p.astype(BF16), v, preferred_element_type=F32)


def _attn_prompt_body(q_ref, k_ref, v_ref, o_ref, *, scale):
    o = _attend(q_ref[...], k_ref[...].astype(BF16), v_ref[...].astype(BF16), scale)
    o_ref[...] = o.astype(o_ref.dtype)


def _attn_prompt(q, k, v, *, n_seq, seq, n_heads, tq):
    rows, d = q.shape
    hd = d // n_heads
    n_mem = k.shape[0] // n_seq
    nt = seq // tq
    kv = pl.BlockSpec((n_mem, hd), lambda b, h, i: (b, h))
    qo = pl.BlockSpec((tq, hd), lambda b, h, i: (b * nt + i, h))
    return pl.pallas_call(
        functools.partial(_attn_prompt_body, scale=hd ** -0.5),
        out_shape=_sds((rows, d), BF16),
        grid=(n_seq, n_heads, nt),
        in_specs=[qo, kv, kv],
        out_specs=qo,
        compiler_params=_params("parallel", "parallel", "parallel"),
        name="attn_prompt",
    )(q, k, v)


def _attn_sample_body(q_ref, k_ref, v_ref, o_ref, *, scale, hd, steps):
    rows, d = q_ref.shape[1:]
    row_head = lax.broadcasted_iota(jnp.int32, (rows, d), 0) // steps
    col_head = lax.broadcasted_iota(jnp.int32, (rows, d), 1) // hd
    own = row_head == col_head
    for b in range(q_ref.shape[0]):
        q = jnp.where(own, q_ref[b], 0.0).astype(BF16)
        o = _attend(q, k_ref[b], v_ref[b], scale)
        o_ref[b] = jnp.where(own, o, 0.0).astype(o_ref.dtype)


def _attn_sample(q, k, v, *, n_heads, steps, bb):
    nb, rows, d = q.shape
    n_mem = k.shape[1]
    qo = pl.BlockSpec((bb, rows, d), lambda i: (i, 0, 0))
    kv = pl.BlockSpec((bb, n_mem, d), lambda i: (i, 0, 0))
    return pl.pallas_call(
        functools.partial(_attn_sample_body, scale=(d // n_heads) ** -0.5, hd=d // n_heads, steps=steps),
        out_shape=_sds((nb, rows, d), BF16),
        grid=(nb // bb,),
        in_specs=[qo, kv, kv],
        out_specs=qo,
        compiler_params=_params("parallel"),
        name="attn_sample",
    )(q, k, v)


def kernel(x_prompt, x_sample, cache_mem_k, cache_mem_v, state_conv_a, state_conv_c, mem_prompt, norm_mix, w_in, b_gate, conv_a_w, ln_b_g, ln_b_b, w_s, b_s, conv_c_w, conv_c_b, ln_c_g, ln_c_b, w_branch, w_mix_out, norm_xattn, norm_mem, w_q, w_k, w_v, w_o, norm_ffn, w_gate, w_up, w_down, norm_final):
    n_seq, seq, d = x_prompt.shape
    nb, steps, _ = x_sample.shape
    depth = w_in.shape[0]
    dbr = conv_a_w.shape[-1]
    taps_a, taps_c = conv_a_w.shape[1], conv_c_w.shape[1]
    groups, chunk = w_s.shape[1], w_s.shape[2]
    gd = dbr // groups
    n_mem, n_heads, hd = cache_mem_k.shape[2:]
    mp = n_seq * seq
    ms = nb * steps
    m = mp + ms
    assert steps <= chunk and seq % chunk == 0 and mp % ms == 0 and n_heads * hd == d
    col = {name: i * dbr for i, name in enumerate(("a_h", "a_c", "a_b", "b_u", "b_v", "c_val", "c_gate"))}
    gate_col0 = 7 * dbr

    dff = w_gate.shape[-1]
    bm = _tile(m, 1088, BF16_ROWS)
    bm_half = _tile(m, 544, BF16_ROWS)
    bm_norm = _tile(math.gcd(mp, ms), 512, BF16_ROWS)
    bm_mem = _tile(n_seq * n_mem, 1024, BF16_ROWS)
    bn_d = _tile(d, 512, LANES)
    bn_in = _tile(w_in.shape[-1], 512, LANES)
    bn_gate = _tile(math.gcd(d, gate_col0), 512, LANES)
    bn_ff = _tile(dff, 256, LANES)
    bn_down = _tile(d, 256, LANES)
    rows_conv = _tile(seq, 256, SUBLANES)
    cb_a = dbr
    cb_c = _tile(dbr, 512, LANES)
    tq = _tile(seq, 512, BF16_ROWS)
    bb = _tile(nb, 2, 1)
    qrows = -(-n_heads * steps // BF16_ROWS) * BF16_ROWS

    def tm(a):
        return jnp.swapaxes(a, 0, 1).reshape(a.shape[1] * nb, a.shape[2])

    def bmaj(a, t):
        return jnp.swapaxes(a.reshape(t, nb, a.shape[1]), 0, 1)

    x = jnp.concatenate([x_prompt.reshape(mp, d), tm(x_sample)], axis=0)
    mem = mem_prompt.reshape(n_seq * n_mem, d)
    w_branch_bf = _cast_bf16(w_branch)
    w_down_bf = _cast_bf16(w_down)

    outs = {k: [] for k in ("mk", "mv", "ca_p", "cc_p", "ca_s", "cc_s", "cv_s")}
    for l in range(depth):
        h = _rmsnorm(x, norm_mix[l], BF16, bm=bm_norm)
        z = _matmul(h, w_in, l, F32, bm=bm, bn=bn_in)

        ya, ca_p = _conv_prompt(z, (col["a_h"], col["a_c"], col["a_b"]), conv_a_w[l], m, BF16,
                                n_seq=n_seq, seq=seq, rows=rows_conv, cb=cb_a, gated=True)
        ya, ca_s = _conv_sample(z, (col["a_h"], col["a_c"], col["a_b"]), conv_a_w[l], tm(state_conv_a[l]), ya,
                                row0=mp, steps=steps, nb=nb, cb=_tile(dbr, 512, LANES), gated=True)

        bias_exp = jnp.repeat(b_s[l].T, gd, axis=1)
        wexp = jnp.repeat(w_s[l][:, :steps, :steps].transpose(1, 2, 0).reshape(steps * steps, groups), gd, axis=1)
        yb = _sgu_prompt(z, col["b_u"], col["b_v"], w_s[l], bias_exp, ln_b_g[l], ln_b_b[l], m, n_rows=mp)
        yb, cv_s = _sgu_sample(z, col["b_u"], col["b_v"], wexp, bias_exp[:steps], ln_b_g[l], ln_b_b[l], yb,
                               row0=mp, steps=steps, nb=nb)

        cc, cc_p = _conv_prompt(z, (col["c_val"], col["c_gate"]), conv_c_w[l], m, F32,
                                n_seq=n_seq, seq=seq, rows=rows_conv, cb=cb_c, gated=False)
        cc, cc_s = _conv_sample(z, (col["c_val"], col["c_gate"]), conv_c_w[l], tm(state_conv_c[l]), cc,
                                row0=mp, steps=steps, nb=nb, cb=_tile(dbr, 256, LANES), gated=False)
        yc = _ln_silu(cc, conv_c_b[l], ln_c_g[l], ln_c_b[l], bm=bm_norm)

        merged = _branch_merge(ya, yb, yc, w_branch_bf, l, z, gate_col0, b_gate[l], bm=bm_half, bn=bn_gate)
        x = _matmul_residual(merged, w_mix_out, l, x, bm=bm, bn=bn_d)

        hm = _rmsnorm(mem, norm_mem[l], BF16, bm=_tile(mem.shape[0], 256, BF16_ROWS))
        mk = _matmul(hm, w_k, l, F32, bm=bm_mem, bn=bn_d)
        mv = _matmul(hm, w_v, l, F32, bm=bm_mem, bn=bn_d)
        hq = _rmsnorm(x, norm_xattn[l], BF16, bm=bm_norm)
        q = _matmul(hq, w_q, l, BF16, bm=bm, bn=bn_d)
        o = _attn_prompt(q, mk, mv, n_seq=n_seq, seq=seq, n_heads=n_heads, tq=tq)
        q_s = jnp.tile(bmaj(q[mp:], steps).astype(F32), (1, n_heads, 1))
        q_s = jnp.pad(q_s, ((0, 0), (0, qrows - n_heads * steps), (0, 0)))
        o_s = _attn_sample(q_s, cache_mem_k[l].reshape(nb, n_mem, d).astype(BF16),
                           cache_mem_v[l].reshape(nb, n_mem, d).astype(BF16), n_heads=n_heads, steps=steps, bb=bb)
        o_s = o_s[:, :n_heads * steps].reshape(nb, n_heads, steps, d).sum(axis=1)
        o = lax.dynamic_update_slice(o, tm(o_s), (mp, 0))
        x = _matmul_residual(o, w_o, l, x, bm=bm, bn=bn_d)

        hf = _rmsnorm(x, norm_ffn[l], BF16, bm=bm_norm)
        act = _swiglu_up(hf, w_gate, w_up, l, bm=bm, bn=bn_ff)
        x = _matmul_residual_rows(act, w_down_bf, l, x, bm=bm_half, bn=bn_down)

        outs["mk"].append(mk.reshape(n_seq, n_mem, n_heads, hd))
        outs["mv"].append(mv.reshape(n_seq, n_mem, n_heads, hd))
        outs["ca_p"].append(ca_p)
        outs["cc_p"].append(cc_p)
        outs["ca_s"].append(bmaj(ca_s, taps_a - 1))
        outs["cc_s"].append(bmaj(cc_s, taps_c - 1))
        outs["cv_s"].append(bmaj(cv_s, steps))

    y_prompt = _rmsnorm(x, norm_final, F32, bm=bm_norm, n_blocks=mp // bm_norm).reshape(n_seq, seq, d)
    y_sample = bmaj(_rmsnorm(x, norm_final, F32, bm=bm_norm, row_block0=mp // bm_norm, n_blocks=ms // bm_norm), steps)
    return (y_prompt, y_sample, jnp.stack(outs["mk"]), jnp.stack(outs["mv"]), jnp.stack(outs["ca_p"]),
            jnp.stack(outs["cc_p"]), jnp.stack(outs["ca_s"]), jnp.stack(outs["cc_s"]), jnp.stack(outs["cv_s"]))
```

```python
import functools
import math

import jax
import jax.numpy as jnp
from jax import lax
from jax.experimental import pallas as pl
from jax.experimental.pallas import tpu as pltpu

EPS = 1e-6
F32 = jnp.float32
BF16 = jnp.bfloat16
V7X_VMEM_BYTES = 64 * 2**20
VMEM_LIMIT_BYTES = V7X_VMEM_BYTES - 8 * 2**20
SUBLANES = 8
LANES = 128
BF16_ROWS = 16
ACC_VREGS = 16


def _tile(n, target, mult):
    best = None
    for t in range(mult, min(n, target) + 1, mult):
        if n % t == 0:
            best = t
    assert best is not None, (n, target, mult)
    return best


def _params(*sem):
    return pltpu.CompilerParams(dimension_semantics=sem, vmem_limit_bytes=VMEM_LIMIT_BYTES)


def _sds(shape, dtype):
    return jax.ShapeDtypeStruct(shape, dtype)


def _rms_body(x_ref, g_ref, o_ref):
    x = x_ref[...]
    y = x * lax.rsqrt(jnp.mean(x * x, axis=-1, keepdims=True) + EPS)
    o_ref[...] = (y * g_ref[...]).astype(o_ref.dtype)


def _rmsnorm(x, g, out_dtype, *, bm, row_block0=0, n_blocks=None):
    m, d = x.shape
    n_blocks = m // bm if n_blocks is None else n_blocks
    return pl.pallas_call(
        _rms_body,
        out_shape=_sds((n_blocks * bm, d), out_dtype),
        grid=(n_blocks,),
        in_specs=[pl.BlockSpec((bm, d), lambda i: (i + row_block0, 0)),
                  pl.BlockSpec((1, d), lambda i: (0, 0))],
        out_specs=pl.BlockSpec((bm, d), lambda i: (i, 0)),
        compiler_params=_params("parallel"),
        name="rmsnorm",
    )(x, g.reshape(1, d))


def _ln_silu_body(c_ref, cb_ref, g_ref, b_ref, o_ref):
    c = c_ref[...] + cb_ref[...]
    mu = jnp.mean(c, axis=-1, keepdims=True)
    cc = c - mu
    var = jnp.mean(cc * cc, axis=-1, keepdims=True)
    y = cc * lax.rsqrt(var + EPS) * g_ref[...] + b_ref[...]
    o_ref[...] = (y * jax.nn.sigmoid(y)).astype(o_ref.dtype)


def _ln_silu(c, conv_bias, g, b, *, bm):
    m, d = c.shape
    vec = pl.BlockSpec((1, d), lambda i: (0, 0))
    return pl.pallas_call(
        _ln_silu_body,
        out_shape=_sds((m, d), BF16),
        grid=(m // bm,),
        in_specs=[pl.BlockSpec((bm, d), lambda i: (i, 0)), vec, vec, vec],
        out_specs=pl.BlockSpec((bm, d), lambda i: (i, 0)),
        compiler_params=_params("parallel"),
        name="ln_silu",
    )(c, conv_bias.reshape(1, d), g.reshape(1, d), b.reshape(1, d))


def _cast_body(w_ref, o_ref):
    o_ref[...] = w_ref[...].astype(o_ref.dtype)


def _cast_bf16(w):
    n = w.shape[-1]
    rows = math.prod(w.shape[:-1])
    br = _tile(rows, 512, BF16_ROWS)
    out = pl.pallas_call(
        _cast_body,
        out_shape=_sds((rows, n), BF16),
        grid=(rows // br,),
        in_specs=[pl.BlockSpec((br, n), lambda i: (i, 0))],
        out_specs=pl.BlockSpec((br, n), lambda i: (i, 0)),
        compiler_params=_params("parallel"),
        name="cast_bf16",
    )(w.reshape(rows, n))
    return out.reshape(w.shape)


def _ws_maps(nj):
    row = lambda jj, i: jnp.where(jj > 0, i, 0)
    col = lambda jj: jnp.maximum(jj - 1, 0)
    nxt = lambda jj: jnp.minimum(jj, nj - 1)
    return row, col, nxt


def _round_chunk(w_ref, wbf_ref):
    ck = w_ref.shape[0]
    r0 = pl.multiple_of(pl.program_id(1) * ck, ck)
    wbf_ref[pl.program_id(0) % 2, pl.ds(r0, ck), :] = w_ref[...].astype(BF16)


def _ready(wbf_ref):
    return wbf_ref[(pl.program_id(0) + 1) % 2]


def _mm_body(x_ref, w_ref, o_ref, wbf_ref):
    _round_chunk(w_ref, wbf_ref)

    @pl.when(pl.program_id(0) > 0)
    def _():
        o_ref[...] = jnp.dot(x_ref[...], _ready(wbf_ref), preferred_element_type=F32).astype(o_ref.dtype)


def _matmul(x, w, layer, out_dtype, *, bm, bn):
    m, k = x.shape
    n = w.shape[2]
    nj, ni = n // bn, m // bm
    ck = k // ni
    assert ck * ni == k and ck % BF16_ROWS == 0
    row, col, nxt = _ws_maps(nj)
    return pl.pallas_call(
        _mm_body,
        out_shape=_sds((m, n), out_dtype),
        grid=(nj + 1, ni),
        in_specs=[pl.BlockSpec((bm, k), lambda jj, i: (row(jj, i), 0)),
                  pl.BlockSpec((None, ck, bn), lambda jj, i: (layer, i, nxt(jj)))],
        out_specs=pl.BlockSpec((bm, bn), lambda jj, i: (row(jj, i), col(jj))),
        scratch_shapes=[pltpu.VMEM((2, k, bn), BF16)],
        compiler_params=_params("arbitrary", "arbitrary"),
        name="matmul",
    )(x, w)


def _mm_res_body(x_ref, w_ref, r_ref, o_ref, wbf_ref):
    _round_chunk(w_ref, wbf_ref)

    @pl.when(pl.program_id(0) > 0)
    def _():
        o_ref[...] = r_ref[...] + jnp.dot(x_ref[...], _ready(wbf_ref), preferred_element_type=F32)


def _matmul_residual(x, w, layer, res, *, bm, bn):
    m, k = x.shape
    n = w.shape[2]
    nj, ni = n // bn, m // bm
    ck = k // ni
    assert ck * ni == k and ck % BF16_ROWS == 0
    row, col, nxt = _ws_maps(nj)
    ij = lambda jj, i: (row(jj, i), col(jj))
    return pl.pallas_call(
        _mm_res_body,
        out_shape=_sds((m, n), F32),
        grid=(nj + 1, ni),
        in_specs=[pl.BlockSpec((bm, k), lambda jj, i: (row(jj, i), 0)),
                  pl.BlockSpec((None, ck, bn), lambda jj, i: (layer, i, nxt(jj))),
                  pl.BlockSpec((bm, bn), ij)],
        out_specs=pl.BlockSpec((bm, bn), ij),
        scratch_shapes=[pltpu.VMEM((2, k, bn), BF16)],
        input_output_aliases={2: 0},
        compiler_params=_params("arbitrary", "arbitrary"),
        name="matmul_residual",
    )(x, w, res)


def _mm_res_rows_body(x_ref, w_ref, r_ref, o_ref):
    o_ref[...] = r_ref[...] + jnp.dot(x_ref[...], w_ref[...], preferred_element_type=F32)


def _matmul_residual_rows(x, w, layer, res, *, bm, bn):
    m, k = x.shape
    n = w.shape[2]
    ij = lambda i, j: (i, j)
    return pl.pallas_call(
        _mm_res_rows_body,
        out_shape=_sds((m, n), F32),
        grid=(m // bm, n // bn),
        in_specs=[pl.BlockSpec((bm, k), lambda i, j: (i, 0)),
                  pl.BlockSpec((None, k, bn), lambda i, j: (layer, 0, j)),
                  pl.BlockSpec((bm, bn), ij)],
        out_specs=pl.BlockSpec((bm, bn), ij),
        input_output_aliases={2: 0},
        compiler_params=_params("parallel", "parallel"),
        name="matmul_residual_rows",
    )(x, w, res)


def _branch_body(ya_ref, yb_ref, yc_ref, wa_ref, wb_ref, wc_ref, ga_ref, gb_ref, gc_ref, bg_ref, o_ref,
                 wabf_ref, wbbf_ref, wcbf_ref):
    branches = ((ya_ref, wa_ref, ga_ref, wabf_ref), (yb_ref, wb_ref, gb_ref, wbbf_ref), (yc_ref, wc_ref, gc_ref, wcbf_ref))
    for _, w_ref, _, wbf_ref in branches:
        _round_chunk(w_ref, wbf_ref)

    @pl.when(pl.program_id(0) > 0)
    def _():
        acc = None
        for k, (y_ref, _, g_ref, wbf_ref) in enumerate(branches):
            d = jnp.dot(y_ref[...], _ready(wbf_ref), preferred_element_type=F32)
            t = jax.nn.sigmoid(g_ref[...] + bg_ref[k:k + 1, :]) * d
            acc = t if acc is None else acc + t
        o_ref[...] = acc.astype(o_ref.dtype)


def _branch_merge(ya, yb, yc, w_branch, layer, z, gate_col0, b_gate, *, bm, bn):
    m, dbr = ya.shape
    n = w_branch.shape[3]
    nj, ni = n // bn, m // bm
    ck = dbr // ni
    assert ck * ni == dbr and ck % BF16_ROWS == 0
    gblk0 = gate_col0 // bn
    row, col, nxt = _ws_maps(nj)
    yspec = pl.BlockSpec((bm, dbr), lambda jj, i: (row(jj, i), 0))

    def wspec(k):
        return pl.BlockSpec((None, None, ck, bn), lambda jj, i: (layer, k, i, nxt(jj)))

    def gspec(k):
        return pl.BlockSpec((bm, bn), lambda jj, i: (row(jj, i), gblk0 + k * nj + col(jj)))

    return pl.pallas_call(
        _branch_body,
        out_shape=_sds((m, n), BF16),
        grid=(nj + 1, ni),
        in_specs=[yspec, yspec, yspec, wspec(0), wspec(1), wspec(2), gspec(0), gspec(1), gspec(2),
                  pl.BlockSpec((3, bn), lambda jj, i: (0, col(jj)))],
        out_specs=pl.BlockSpec((bm, bn), lambda jj, i: (row(jj, i), col(jj))),
        scratch_shapes=[pltpu.VMEM((2, dbr, bn), BF16)] * 3,
        compiler_params=_params("arbitrary", "arbitrary"),
        name="branch_merge",
    )(ya, yb, yc, w_branch, w_branch, w_branch, z, z, z, b_gate)


def _swiglu_body(x_ref, wg_ref, wu_ref, o_ref, wgbf_ref, wubf_ref):
    _round_chunk(wg_ref, wgbf_ref)
    _round_chunk(wu_ref, wubf_ref)

    @pl.when(pl.program_id(0) > 0)
    def _():
        x = x_ref[...]
        g = jnp.dot(x, _ready(wgbf_ref), preferred_element_type=F32)
        u = jnp.dot(x, _ready(wubf_ref), preferred_element_type=F32)
        o_ref[...] = (g * jax.nn.sigmoid(g) * u).astype(o_ref.dtype)


def _swiglu_up(x, wg, wu, layer, *, bm, bn):
    m, k = x.shape
    n = wg.shape[2]
    nj, ni = n // bn, m // bm
    ck = k // ni
    assert ck * ni == k and ck % BF16_ROWS == 0
    row, col, nxt = _ws_maps(nj)
    wspec = pl.BlockSpec((None, ck, bn), lambda jj, i: (layer, i, nxt(jj)))
    return pl.pallas_call(
        _swiglu_body,
        out_shape=_sds((m, n), BF16),
        grid=(nj + 1, ni),
        in_specs=[pl.BlockSpec((bm, k), lambda jj, i: (row(jj, i), 0)), wspec, wspec],
        out_specs=pl.BlockSpec((bm, bn), lambda jj, i: (row(jj, i), col(jj))),
        scratch_shapes=[pltpu.VMEM((2, k, bn), BF16), pltpu.VMEM((2, k, bn), BF16)],
        compiler_params=_params("arbitrary", "arbitrary"),
        name="swiglu_up",
    )(x, wg, wu)


def _halo_rows(k):
    return -(-(k - 1) // SUBLANES) * SUBLANES


def _conv_prompt_body(*refs, taps, rows, gated, shifted):
    if gated:
        h_ref, c_ref, b_ref, w_ref, y_ref, st_ref, ext = refs[:7]
        p = c_ref[...] * h_ref[...]
    else:
        h_ref, c_ref, w_ref, y_ref, st_ref, ext = refs[:6]
        p = h_ref[...] * jax.nn.sigmoid(c_ref[...])
    ph = refs[-1] if shifted else None
    halo = _halo_rows(taps)
    cb = ext.shape[1]

    @pl.when(pl.program_id(2) == 0)
    def _():
        ext[0:halo, :] = jnp.zeros((halo, cb), F32)

    ext[halo:halo + rows, :] = p
    if shifted:
        n = halo + rows - SUBLANES
        for s in range(1, SUBLANES):
            ph[s - 1, 0:n, :] = ext[s:s + n, :]

    def window(off, r0, nr):
        a, s = divmod(off, SUBLANES)
        if s == 0 or not shifted:
            return ext[off + r0:off + r0 + nr, :]
        return ph[s - 1, a * SUBLANES + r0:a * SUBLANES + r0 + nr, :]

    nr = max(SUBLANES, ACC_VREGS * SUBLANES * LANES // cb)
    for r0 in range(0, rows, nr):
        conv = None
        for k in range(taps):
            term = w_ref[k:k + 1, :] * window(halo - (taps - 1) + k, r0, nr)
            conv = term if conv is None else conv + term
        if gated:
            conv = b_ref[r0:r0 + nr, :] * conv
        y_ref[r0:r0 + nr, :] = conv.astype(y_ref.dtype)
    st_ref[...] = ext[halo + rows - (taps - 1):halo + rows, :]
    ext[0:halo, :] = ext[rows:rows + halo, :]


def _conv_prompt(z, col_blocks, w, out_rows, out_dtype, *, n_seq, seq, rows, cb, gated):
    taps, dbr = w.shape
    nt = seq // rows
    ncb = dbr // cb
    halo = _halo_rows(taps)
    shifted = taps - 1 > SUBLANES
    assert rows % max(SUBLANES, ACC_VREGS * SUBLANES * LANES // cb) == 0

    def zspec(col0):
        return pl.BlockSpec((rows, cb), lambda b, c, i: (b * nt + i, col0 // cb + c))

    in_specs = [zspec(c0) for c0 in col_blocks] + [pl.BlockSpec((taps, cb), lambda b, c, i: (0, c))]
    scratch = [pltpu.VMEM((halo + rows, cb), F32)]
    if shifted:
        scratch.append(pltpu.VMEM((SUBLANES - 1, halo + rows, cb), F32))
    y, st = pl.pallas_call(
        functools.partial(_conv_prompt_body, taps=taps, rows=rows, gated=gated, shifted=shifted),
        out_shape=(_sds((out_rows, dbr), out_dtype), _sds((n_seq, taps - 1, dbr), F32)),
        grid=(n_seq, ncb, nt),
        in_specs=in_specs,
        out_specs=(pl.BlockSpec((rows, cb), lambda b, c, i: (b * nt + i, c)),
                   pl.BlockSpec((None, taps - 1, cb), lambda b, c, i: (b, 0, c))),
        scratch_shapes=scratch,
        compiler_params=_params("arbitrary", "arbitrary", "arbitrary"),
        name="conv_prompt",
    )(*([z] * len(col_blocks)), w)
    return y, st


def _conv_sample_body(*refs, taps, steps, nb, gated):
    if gated:
        h_ref, c_ref, b_ref, st_ref, w_ref, ybuf_ref, y_ref, nst_ref, p_ref = refs
        p_ref[...] = c_ref[...] * h_ref[...]
    else:
        h_ref, c_ref, st_ref, w_ref, ybuf_ref, y_ref, nst_ref, p_ref = refs
        p_ref[...] = h_ref[...] * jax.nn.sigmoid(c_ref[...])
    del ybuf_ref

    def slab(t):
        return slice(t * nb, (t + 1) * nb)

    def ext(e):
        return st_ref[slab(e), :] if e < taps - 1 else p_ref[slab(e - (taps - 1)), :]

    for t in range(steps):
        conv = None
        for k in range(taps):
            term = w_ref[k:k + 1, :] * ext(t + k)
            conv = term if conv is None else conv + term
        if gated:
            conv = b_ref[slab(t), :] * conv
        y_ref[slab(t), :] = conv.astype(y_ref.dtype)
    for j in range(taps - 1):
        nst_ref[slab(j), :] = ext(steps + j)


def _conv_sample(z, col_blocks, w, state_tm, ybuf, *, row0, steps, nb, cb, gated):
    taps, dbr = w.shape
    ncb = dbr // cb
    srows = steps * nb
    rblk = row0 // srows

    def zspec(col0):
        return pl.BlockSpec((srows, cb), lambda c: (rblk, col0 // cb + c))

    in_specs = ([zspec(c0) for c0 in col_blocks]
                + [pl.BlockSpec(((taps - 1) * nb, cb), lambda c: (0, c)),
                   pl.BlockSpec((taps, cb), lambda c: (0, c)),
                   pl.BlockSpec(memory_space=pl.ANY)])
    n_in = len(in_specs)
    return pl.pallas_call(
        functools.partial(_conv_sample_body, taps=taps, steps=steps, nb=nb, gated=gated),
        out_shape=(_sds(ybuf.shape, ybuf.dtype), _sds(((taps - 1) * nb, dbr), F32)),
        grid=(ncb,),
        in_specs=in_specs,
        out_specs=(pl.BlockSpec((srows, cb), lambda c: (rblk, c)),
                   pl.BlockSpec(((taps - 1) * nb, cb), lambda c: (0, c))),
        input_output_aliases={n_in - 1: 0},
        scratch_shapes=[pltpu.VMEM((srows, cb), F32)],
        compiler_params=_params("parallel"),
        name="conv_sample",
    )(*([z] * len(col_blocks)), state_tm, w, ybuf)


def _gelu_ln(x, g, b):
    x = jax.nn.gelu(x)
    mu = jnp.mean(x, axis=-1, keepdims=True)
    xc = x - mu
    var = jnp.mean(xc * xc, axis=-1, keepdims=True)
    return xc * lax.rsqrt(var + EPS) * g + b


def _sgu_prompt_body(u_ref, v_ref, ws_ref, bias_ref, g_ref, b_ref, y_ref, *, groups, gd):
    chunk = u_ref.shape[0]
    v = _gelu_ln(v_ref[...], g_ref[...], b_ref[...])
    row = lax.broadcasted_iota(jnp.int32, (chunk, chunk), 0)
    col = lax.broadcasted_iota(jnp.int32, (chunk, chunk), 1)
    causal = col <= row
    for g in range(groups):
        cols = slice(g * gd, (g + 1) * gd)
        wg = jnp.where(causal, ws_ref[g], 0.0).astype(BF16)
        s = jnp.dot(wg, v[:, cols].astype(BF16), preferred_element_type=F32) + bias_ref[:, cols]
        y_ref[:, cols] = (jax.nn.gelu(u_ref[:, cols]) * s).astype(y_ref.dtype)


def _sgu_prompt(z, u_col0, v_col0, w_s, bias_exp, ln_g, ln_b, out_rows, *, n_rows):
    groups, chunk, _ = w_s.shape
    dbr = bias_exp.shape[1]
    vec = pl.BlockSpec((1, dbr), lambda i: (0, 0))
    return pl.pallas_call(
        functools.partial(_sgu_prompt_body, groups=groups, gd=dbr // groups),
        out_shape=_sds((out_rows, dbr), BF16),
        grid=(n_rows // chunk,),
        in_specs=[pl.BlockSpec((chunk, dbr), lambda i: (i, u_col0 // dbr)),
                  pl.BlockSpec((chunk, dbr), lambda i: (i, v_col0 // dbr)),
                  pl.BlockSpec((groups, chunk, chunk), lambda i: (0, 0, 0)),
                  pl.BlockSpec((chunk, dbr), lambda i: (0, 0)), vec, vec],
        out_specs=pl.BlockSpec((chunk, dbr), lambda i: (i, 0)),
        compiler_params=_params("parallel"),
        name="sgu_prompt",
    )(z, z, w_s, bias_exp, ln_g.reshape(1, dbr), ln_b.reshape(1, dbr))


def _sgu_sample_body(u_ref, v_ref, wexp_ref, bexp_ref, g_ref, b_ref, ybuf_ref, y_ref, vout_ref, *, steps, nb):
    del ybuf_ref
    v = _gelu_ln(v_ref[...], g_ref[...], b_ref[...])
    vout_ref[...] = v
    for i in range(steps):
        s = bexp_ref[i:i + 1, :]
        for j in range(i + 1):
            s = s + wexp_ref[i * steps + j:i * steps + j + 1, :] * v[j * nb:(j + 1) * nb, :]
        rows = slice(i * nb, (i + 1) * nb)
        y_ref[rows, :] = (jax.nn.gelu(u_ref[rows, :]) * s).astype(y_ref.dtype)


def _sgu_sample(z, u_col0, v_col0, wexp, bexp, ln_g, ln_b, ybuf, *, row0, steps, nb):
    dbr = bexp.shape[1]
    srows = steps * nb
    rblk = row0 // srows
    vec = pl.BlockSpec((1, dbr), lambda i: (0, 0))
    return pl.pallas_call(
        functools.partial(_sgu_sample_body, steps=steps, nb=nb),
        out_shape=(_sds(ybuf.shape, ybuf.dtype), _sds((srows, dbr), F32)),
        grid=(1,),
        in_specs=[pl.BlockSpec((srows, dbr), lambda i: (rblk, u_col0 // dbr)),
                  pl.BlockSpec((srows, dbr), lambda i: (rblk, v_col0 // dbr)),
                  pl.BlockSpec((steps * steps, dbr), lambda i: (0, 0)),
                  pl.BlockSpec((steps, dbr), lambda i: (0, 0)), vec, vec,
                  pl.BlockSpec(memory_space=pl.ANY)],
        out_specs=(pl.BlockSpec((srows, dbr), lambda i: (rblk, 0)),
                   pl.BlockSpec((srows, dbr), lambda i: (0, 0))),
        input_output_aliases={6: 0},
        compiler_params=_params("arbitrary"),
        name="sgu_sample",
    )(z, z, wexp, bexp, ln_g.reshape(1, dbr), ln_b.reshape(1, dbr), ybuf)


def _attend(q, k, v, scale):
    s = lax.dot_general(q, k, (((1,), (1,)), ((), ())), preferred_element_type=F32) * scale
    e = jnp.exp(s - jnp.max(s, axis=-1, keepdims=True))
    p = e / jnp.sum(e, axis=-1, keepdims=True)
    return jnp.dot(p.astype(BF16), v, preferred_element_type=F32)


def _attn_prompt_body(q_ref, k_ref, v_ref, o_ref, *, scale):
    o = _attend(q_ref[...], k_ref[...].astype(BF16), v_ref[...].astype(BF16), scale)
    o_ref[...] = o.astype(o_ref.dtype)


def _attn_prompt(q, k, v, *, n_seq, seq, n_heads, tq):
    rows, d = q.shape
    hd = d // n_heads
    n_mem = k.shape[0] // n_seq
    nt = seq // tq
    kv = pl.BlockSpec((n_mem, hd), lambda b, h, i: (b, h))
    qo = pl.BlockSpec((tq, hd), lambda b, h, i: (b * nt + i, h))
    return pl.pallas_call(
        functools.partial(_attn_prompt_body, scale=hd ** -0.5),
        out_shape=_sds((rows, d), BF16),
        grid=(n_seq, n_heads, nt),
        in_specs=[qo, kv, kv],
        out_specs=qo,
        compiler_params=_params("parallel", "parallel", "parallel"),
        name="attn_prompt",
    )(q, k, v)


def _attn_sample_body(q_ref, k_hbm, v_hbm, o_ref, kbuf, vbuf, sem, *, layer, scale, hd, steps, n_heads):
    i = pl.program_id(0)

    def fetch(b, slot):
        copies = []
        for h in range(n_heads):
            cols = pl.ds(h * hd, hd)
            copies.append(pltpu.make_async_copy(k_hbm.at[layer, b, :, h, :], kbuf.at[slot, :, cols], sem.at[0, slot]))
            copies.append(pltpu.make_async_copy(v_hbm.at[layer, b, :, h, :], vbuf.at[slot, :, cols], sem.at[1, slot]))
        return copies

    @pl.when(i == 0)
    def _():
        for c in fetch(0, 0):
            c.start()

    @pl.when(i + 1 < pl.num_programs(0))
    def _():
        for c in fetch(i + 1, (i + 1) % 2):
            c.start()

    slot = i % 2
    for c in fetch(i, slot):
        c.wait()
    rows, d = q_ref.shape[1:]
    row_head = lax.broadcasted_iota(jnp.int32, (rows, d), 0) // steps
    col_head = lax.broadcasted_iota(jnp.int32, (rows, d), 1) // hd
    own = row_head == col_head
    q = jnp.where(own, q_ref[0], 0.0).astype(BF16)
    o = _attend(q, kbuf[slot].astype(BF16), vbuf[slot].astype(BF16), scale)
    o_ref[0] = jnp.where(own, o, 0.0).astype(o_ref.dtype)


def _attn_sample(q, k, v, layer, *, steps):
    nb, rows, d = q.shape
    n_mem, n_heads, hd = k.shape[2:]
    qo = pl.BlockSpec((1, rows, d), lambda i: (i, 0, 0))
    anyspec = pl.BlockSpec(memory_space=pl.ANY)
    return pl.pallas_call(
        functools.partial(_attn_sample_body, layer=layer, scale=hd ** -0.5, hd=hd, steps=steps, n_heads=n_heads),
        out_shape=_sds((nb, rows, d), BF16),
        grid=(nb,),
        in_specs=[qo, anyspec, anyspec],
        out_specs=qo,
        scratch_shapes=[pltpu.VMEM((2, n_mem, d), F32), pltpu.VMEM((2, n_mem, d), F32),
                        pltpu.SemaphoreType.DMA((2, 2))],
        compiler_params=_params("arbitrary"),
        name="attn_sample",
    )(q, k, v)


def kernel(x_prompt, x_sample, cache_mem_k, cache_mem_v, state_conv_a, state_conv_c, mem_prompt, norm_mix, w_in, b_gate, conv_a_w, ln_b_g, ln_b_b, w_s, b_s, conv_c_w, conv_c_b, ln_c_g, ln_c_b, w_branch, w_mix_out, norm_xattn, norm_mem, w_q, w_k, w_v, w_o, norm_ffn, w_gate, w_up, w_down, norm_final):
    n_seq, seq, d = x_prompt.shape
    nb, steps, _ = x_sample.shape
    depth = w_in.shape[0]
    dbr = conv_a_w.shape[-1]
    taps_a, taps_c = conv_a_w.shape[1], conv_c_w.shape[1]
    groups, chunk = w_s.shape[1], w_s.shape[2]
    gd = dbr // groups
    n_mem, n_heads, hd = cache_mem_k.shape[2:]
    mp = n_seq * seq
    ms = nb * steps
    m = mp + ms
    assert steps <= chunk and seq % chunk == 0 and mp % ms == 0 and n_heads * hd == d
    col = {name: i * dbr for i, name in enumerate(("a_h", "a_c", "a_b", "b_u", "b_v", "c_val", "c_gate"))}
    gate_col0 = 7 * dbr

    dff = w_gate.shape[-1]
    bm = _tile(m, 1088, BF16_ROWS)
    bm_half = _tile(m, 544, BF16_ROWS)
    bm_norm = _tile(math.gcd(mp, ms), 512, BF16_ROWS)
    bm_mem = _tile(n_seq * n_mem, 1024, BF16_ROWS)
    bn_d = _tile(d, 1024, LANES)
    bn_mem = _tile(d, 512, LANES)
    bn_in = _tile(w_in.shape[-1], 1024, LANES)
    bn_gate = _tile(math.gcd(d, gate_col0), 512, LANES)
    bn_ff = _tile(dff, 256, LANES)
    bn_down = _tile(d, 256, LANES)
    rows_conv = _tile(seq, 256, SUBLANES)
    cb_a = dbr
    cb_c = _tile(dbr, 512, LANES)
    tq = _tile(seq, 512, BF16_ROWS)
    qrows = -(-n_heads * steps // BF16_ROWS) * BF16_ROWS

    def tm(a):
        return jnp.swapaxes(a, 0, 1).reshape(a.shape[1] * nb, a.shape[2])

    def bmaj(a, t):
        return jnp.swapaxes(a.reshape(t, nb, a.shape[1]), 0, 1)

    x = jnp.concatenate([x_prompt.reshape(mp, d), tm(x_sample)], axis=0)
    mem = mem_prompt.reshape(n_seq * n_mem, d)
    w_down_bf = _cast_bf16(w_down)

    outs = {k: [] for k in ("mk", "mv", "ca_p", "cc_p", "ca_s", "cc_s", "cv_s")}
    for l in range(depth):
        h = _rmsnorm(x, norm_mix[l], BF16, bm=bm_norm)
        z = _matmul(h, w_in, l, F32, bm=bm, bn=bn_in)

        ya, ca_p = _conv_prompt(z, (col["a_h"], col["a_c"], col["a_b"]), conv_a_w[l], m, BF16,
                                n_seq=n_seq, seq=seq, rows=rows_conv, cb=cb_a, gated=True)
        ya, ca_s = _conv_sample(z, (col["a_h"], col["a_c"], col["a_b"]), conv_a_w[l], tm(state_conv_a[l]), ya,
                                row0=mp, steps=steps, nb=nb, cb=_tile(dbr, 512, LANES), gated=True)

        bias_exp = jnp.repeat(b_s[l].T, gd, axis=1)
        wexp = jnp.repeat(w_s[l][:, :steps, :steps].transpose(1, 2, 0).reshape(steps * steps, groups), gd, axis=1)
        yb = _sgu_prompt(z, col["b_u"], col["b_v"], w_s[l], bias_exp, ln_b_g[l], ln_b_b[l], m, n_rows=mp)
        yb, cv_s = _sgu_sample(z, col["b_u"], col["b_v"], wexp, bias_exp[:steps], ln_b_g[l], ln_b_b[l], yb,
                               row0=mp, steps=steps, nb=nb)

        cc, cc_p = _conv_prompt(z, (col["c_val"], col["c_gate"]), conv_c_w[l], m, F32,
                                n_seq=n_seq, seq=seq, rows=rows_conv, cb=cb_c, gated=False)
        cc, cc_s = _conv_sample(z, (col["c_val"], col["c_gate"]), conv_c_w[l], tm(state_conv_c[l]), cc,
                                row0=mp, steps=steps, nb=nb, cb=_tile(dbr, 256, LANES), gated=False)
        yc = _ln_silu(cc, conv_c_b[l], ln_c_g[l], ln_c_b[l], bm=bm_norm)

        merged = _branch_merge(ya, yb, yc, w_branch, l, z, gate_col0, b_gate[l], bm=bm_half, bn=bn_gate)
        x = _matmul_residual(merged, w_mix_out, l, x, bm=bm_half, bn=bn_d)

        hm = _rmsnorm(mem, norm_mem[l], BF16, bm=_tile(mem.shape[0], 256, BF16_ROWS))
        mk = _matmul(hm, w_k, l, F32, bm=bm_mem, bn=bn_mem)
        mv = _matmul(hm, w_v, l, F32, bm=bm_mem, bn=bn_mem)
        hq = _rmsnorm(x, norm_xattn[l], BF16, bm=bm_norm)
        q = _matmul(hq, w_q, l, BF16, bm=bm, bn=bn_d)
        o = _attn_prompt(q, mk, mv, n_seq=n_seq, seq=seq, n_heads=n_heads, tq=tq)
        q_s = jnp.tile(bmaj(q[mp:], steps).astype(F32), (1, n_heads, 1))
        q_s = jnp.pad(q_s, ((0, 0), (0, qrows - n_heads * steps), (0, 0)))
        o_s = _attn_sample(q_s, cache_mem_k, cache_mem_v, l, steps=steps)
        o_s = o_s[:, :n_heads * steps].reshape(nb, n_heads, steps, d).sum(axis=1)
        o = lax.dynamic_update_slice(o, tm(o_s), (mp, 0))
        x = _matmul_residual(o, w_o, l, x, bm=bm_half, bn=bn_d)

        hf = _rmsnorm(x, norm_ffn[l], BF16, bm=bm_norm)
        act = _swiglu_up(hf, w_gate, w_up, l, bm=bm, bn=bn_ff)
        x = _matmul_residual_rows(act, w_down_bf, l, x, bm=bm_half, bn=bn_down)

        outs["mk"].append(mk.reshape(n_seq, n_mem, n_heads, hd))
        outs["mv"].append(mv.reshape(n_seq, n_mem, n_heads, hd))
        outs["ca_p"].append(ca_p)
        outs["cc_p"].append(cc_p)
        outs["ca_s"].append(bmaj(ca_s, taps_a - 1))
        outs["cc_s"].append(bmaj(cc_s, taps_c - 1))
        outs["cv_s"].append(bmaj(cv_s, steps))

    y_prompt = _rmsnorm(x, norm_final, F32, bm=bm_norm, n_blocks=mp // bm_norm).reshape(n_seq, seq, d)
    y_sample = bmaj(_rmsnorm(x, norm_final, F32, bm=bm_norm, row_block0=mp // bm_norm, n_blocks=ms // bm_norm), steps)
    return (y_prompt, y_sample, jnp.stack(outs["mk"]), jnp.stack(outs["mv"]), jnp.stack(outs["ca_p"]),
            jnp.stack(outs["cc_p"]), jnp.stack(outs["ca_s"]), jnp.stack(outs["cc_s"]), jnp.stack(outs["cv_s"]))
```

```python
import functools
import math

import jax
import jax.numpy as jnp
from jax import lax
from jax.experimental import pallas as pl
from jax.experimental.pallas import tpu as pltpu

EPS = 1e-6
F32 = jnp.float32
BF16 = jnp.bfloat16
V7X_VMEM_BYTES = 64 * 2**20
VMEM_LIMIT_BYTES = V7X_VMEM_BYTES - 8 * 2**20
SUBLANES = 8
LANES = 128
BF16_ROWS = 16
ACC_VREGS = 16


def _tile(n, target, mult):
    best = None
    for t in range(mult, min(n, target) + 1, mult):
        if n % t == 0:
            best = t
    assert best is not None, (n, target, mult)
    return best


def _params(*sem):
    return pltpu.CompilerParams(dimension_semantics=sem, vmem_limit_bytes=VMEM_LIMIT_BYTES)


def _sds(shape, dtype):
    return jax.ShapeDtypeStruct(shape, dtype)


def _rms_body(x_ref, g_ref, o_ref):
    x = x_ref[...]
    y = x * lax.rsqrt(jnp.mean(x * x, axis=-1, keepdims=True) + EPS)
    o_ref[...] = (y * g_ref[...]).astype(o_ref.dtype)


def _rmsnorm(x, g, out_dtype, *, bm, row_block0=0, n_blocks=None):
    m, d = x.shape
    n_blocks = m // bm if n_blocks is None else n_blocks
    return pl.pallas_call(
        _rms_body,
        out_shape=_sds((n_blocks * bm, d), out_dtype),
        grid=(n_blocks,),
        in_specs=[pl.BlockSpec((bm, d), lambda i: (i + row_block0, 0)),
                  pl.BlockSpec((1, d), lambda i: (0, 0))],
        out_specs=pl.BlockSpec((bm, d), lambda i: (i, 0)),
        compiler_params=_params("parallel"),
        name="rmsnorm",
    )(x, g.reshape(1, d))


def _lane_sumsq(x):
    sq = x * x
    acc = sq[:, 0:LANES]
    for k in range(1, x.shape[1] // LANES):
        acc = acc + sq[:, k * LANES:(k + 1) * LANES]
    return acc


def _row_scale(s_ref, d):
    tot = s_ref[0]
    for p in range(1, s_ref.shape[0]):
        tot = tot + s_ref[p]
    return lax.rsqrt(jnp.sum(tot, axis=-1, keepdims=True) / d + EPS)


def _prenorm_body(x_ref, g_ref, h_ref, s_ref):
    x = x_ref[...]
    h_ref[...] = (x * g_ref[...]).astype(h_ref.dtype)
    s_ref[0] = _lane_sumsq(x)


def _prenorm(x, g, *, bm):
    m, d = x.shape
    return pl.pallas_call(
        _prenorm_body,
        out_shape=(_sds((m, d), BF16), _sds((1, m, LANES), F32)),
        grid=(m // bm,),
        in_specs=[pl.BlockSpec((bm, d), lambda i: (i, 0)), pl.BlockSpec((1, d), lambda i: (0, 0))],
        out_specs=(pl.BlockSpec((bm, d), lambda i: (i, 0)), pl.BlockSpec((1, bm, LANES), lambda i: (0, i, 0))),
        compiler_params=_params("parallel"),
        name="prenorm",
    )(x, g.reshape(1, d))


def _ln_silu_body(c_ref, cb_ref, g_ref, b_ref, o_ref):
    c = c_ref[...] + cb_ref[...]
    mu = jnp.mean(c, axis=-1, keepdims=True)
    cc = c - mu
    var = jnp.mean(cc * cc, axis=-1, keepdims=True)
    y = cc * lax.rsqrt(var + EPS) * g_ref[...] + b_ref[...]
    o_ref[...] = (y * jax.nn.sigmoid(y)).astype(o_ref.dtype)


def _ln_silu(c, conv_bias, g, b, *, bm):
    m, d = c.shape
    vec = pl.BlockSpec((1, d), lambda i: (0, 0))
    return pl.pallas_call(
        _ln_silu_body,
        out_shape=_sds((m, d), BF16),
        grid=(m // bm,),
        in_specs=[pl.BlockSpec((bm, d), lambda i: (i, 0)), vec, vec, vec],
        out_specs=pl.BlockSpec((bm, d), lambda i: (i, 0)),
        compiler_params=_params("parallel"),
        name="ln_silu",
    )(c, conv_bias.reshape(1, d), g.reshape(1, d), b.reshape(1, d))


def _cast_body(w_ref, o_ref):
    o_ref[...] = w_ref[...].astype(o_ref.dtype)


def _cast_bf16(w):
    n = w.shape[-1]
    rows = math.prod(w.shape[:-1])
    br = _tile(rows, 512, BF16_ROWS)
    out = pl.pallas_call(
        _cast_body,
        out_shape=_sds((rows, n), BF16),
        grid=(rows // br,),
        in_specs=[pl.BlockSpec((br, n), lambda i: (i, 0))],
        out_specs=pl.BlockSpec((br, n), lambda i: (i, 0)),
        compiler_params=_params("parallel"),
        name="cast_bf16",
    )(w.reshape(rows, n))
    return out.reshape(w.shape)


def _ws_maps(nj):
    row = lambda jj, i: jnp.where(jj > 0, i, 0)
    col = lambda jj: jnp.maximum(jj - 1, 0)
    nxt = lambda jj: jnp.minimum(jj, nj - 1)
    return row, col, nxt


def _round_chunk(w_ref, wbf_ref):
    ck = w_ref.shape[0]
    r0 = pl.multiple_of(pl.program_id(1) * ck, ck)
    wbf_ref[pl.program_id(0) % 2, pl.ds(r0, ck), :] = w_ref[...].astype(BF16)


def _ready(wbf_ref):
    return wbf_ref[(pl.program_id(0) + 1) % 2]


def _mm_body(*refs, scaled):
    if scaled:
        x_ref, w_ref, s_ref, o_ref, wbf_ref = refs
    else:
        x_ref, w_ref, o_ref, wbf_ref = refs
    _round_chunk(w_ref, wbf_ref)

    @pl.when(pl.program_id(0) > 0)
    def _():
        y = jnp.dot(x_ref[...], _ready(wbf_ref), preferred_element_type=F32)
        if scaled:
            y = y * _row_scale(s_ref, x_ref.shape[1])
        o_ref[...] = y.astype(o_ref.dtype)


def _matmul(x, w, layer, out_dtype, *, bm, bn, ssq=None):
    m, k = x.shape
    n = w.shape[2]
    nj, ni = n // bn, m // bm
    ck = k // ni
    assert ck * ni == k and ck % BF16_ROWS == 0
    row, col, nxt = _ws_maps(nj)
    in_specs = [pl.BlockSpec((bm, k), lambda jj, i: (row(jj, i), 0)),
                pl.BlockSpec((None, ck, bn), lambda jj, i: (layer, i, nxt(jj)))]
    args = [x, w]
    if ssq is not None:
        in_specs.append(pl.BlockSpec((ssq.shape[0], bm, LANES), lambda jj, i: (0, row(jj, i), 0)))
        args.append(ssq)
    return pl.pallas_call(
        functools.partial(_mm_body, scaled=ssq is not None),
        out_shape=_sds((m, n), out_dtype),
        grid=(nj + 1, ni),
        in_specs=in_specs,
        out_specs=pl.BlockSpec((bm, bn), lambda jj, i: (row(jj, i), col(jj))),
        scratch_shapes=[pltpu.VMEM((2, k, bn), BF16)],
        compiler_params=_params("arbitrary", "arbitrary"),
        name="matmul",
    )(*args)


def _mm_res_body(x_ref, w_ref, r_ref, g_ref, o_ref, h_ref, s_ref, wbf_ref):
    _round_chunk(w_ref, wbf_ref)

    @pl.when(pl.program_id(0) > 0)
    def _():
        y = r_ref[...] + jnp.dot(x_ref[...], _ready(wbf_ref), preferred_element_type=F32)
        o_ref[...] = y
        h_ref[...] = (y * g_ref[...]).astype(h_ref.dtype)
        s_ref[...] = _lane_sumsq(y)


def _matmul_residual(x, w, layer, res, g_next, *, bm, bn):
    m, k = x.shape
    n = w.shape[2]
    nj, ni = n // bn, m // bm
    ck = k // ni
    assert ck * ni == k and ck % BF16_ROWS == 0
    row, col, nxt = _ws_maps(nj)
    ij = lambda jj, i: (row(jj, i), col(jj))
    return pl.pallas_call(
        _mm_res_body,
        out_shape=(_sds((m, n), F32), _sds((m, n), BF16), _sds((nj, m, LANES), F32)),
        grid=(nj + 1, ni),
        in_specs=[pl.BlockSpec((bm, k), lambda jj, i: (row(jj, i), 0)),
                  pl.BlockSpec((None, ck, bn), lambda jj, i: (layer, i, nxt(jj))),
                  pl.BlockSpec((bm, bn), ij),
                  pl.BlockSpec((1, bn), lambda jj, i: (0, col(jj)))],
        out_specs=(pl.BlockSpec((bm, bn), ij), pl.BlockSpec((bm, bn), ij),
                   pl.BlockSpec((None, bm, LANES), lambda jj, i: (col(jj), row(jj, i), 0))),
        scratch_shapes=[pltpu.VMEM((2, k, bn), BF16)],
        input_output_aliases={2: 0},
        compiler_params=_params("arbitrary", "arbitrary"),
        name="matmul_residual",
    )(x, w, res, g_next.reshape(1, n))


def _mm_res_rows_body(*refs, with_norm):
    if with_norm:
        x_ref, w_ref, r_ref, g_ref, o_ref, h_ref, s_ref = refs
    else:
        x_ref, w_ref, r_ref, o_ref = refs
    y = r_ref[...] + jnp.dot(x_ref[...], w_ref[...], preferred_element_type=F32)
    o_ref[...] = y
    if with_norm:
        h_ref[...] = (y * g_ref[...]).astype(h_ref.dtype)
        part = _lane_sumsq(y)

        @pl.when(pl.program_id(1) == 0)
        def _():
            s_ref[0] = part

        @pl.when(pl.program_id(1) > 0)
        def _():
            s_ref[0] = s_ref[0] + part


def _matmul_residual_rows(x, w, layer, res, g_next, *, bm, bn):
    m, k = x.shape
    n = w.shape[2]
    ij = lambda i, j: (i, j)
    with_norm = g_next is not None
    in_specs = [pl.BlockSpec((bm, k), lambda i, j: (i, 0)),
                pl.BlockSpec((None, k, bn), lambda i, j: (layer, 0, j)),
                pl.BlockSpec((bm, bn), ij)]
    out_shape = [_sds((m, n), F32)]
    out_specs = [pl.BlockSpec((bm, bn), ij)]
    args = [x, w, res]
    if with_norm:
        in_specs.append(pl.BlockSpec((1, bn), lambda i, j: (0, j)))
        args.append(g_next.reshape(1, n))
        out_shape += [_sds((m, n), BF16), _sds((1, m, LANES), F32)]
        out_specs += [pl.BlockSpec((bm, bn), ij), pl.BlockSpec((1, bm, LANES), lambda i, j: (0, i, 0))]
    out = pl.pallas_call(
        functools.partial(_mm_res_rows_body, with_norm=with_norm),
        out_shape=tuple(out_shape),
        grid=(m // bm, n // bn),
        in_specs=in_specs,
        out_specs=tuple(out_specs),
        input_output_aliases={2: 0},
        compiler_params=_params("arbitrary", "arbitrary"),
        name="matmul_residual_rows",
    )(*args)
    return out if with_norm else out[0]


def _branch_body(ya_ref, yb_ref, yc_ref, wa_ref, wb_ref, wc_ref, ga_ref, gb_ref, gc_ref, bg_ref, o_ref,
                 wabf_ref, wbbf_ref, wcbf_ref):
    branches = ((ya_ref, wa_ref, ga_ref, wabf_ref), (yb_ref, wb_ref, gb_ref, wbbf_ref), (yc_ref, wc_ref, gc_ref, wcbf_ref))
    for _, w_ref, _, wbf_ref in branches:
        _round_chunk(w_ref, wbf_ref)

    @pl.when(pl.program_id(0) > 0)
    def _():
        acc = None
        for k, (y_ref, _, g_ref, wbf_ref) in enumerate(branches):
            d = jnp.dot(y_ref[...], _ready(wbf_ref), preferred_element_type=F32)
            t = jax.nn.sigmoid(g_ref[...] + bg_ref[k:k + 1, :]) * d
            acc = t if acc is None else acc + t
        o_ref[...] = acc.astype(o_ref.dtype)


def _branch_merge(ya, yb, yc, w_branch, layer, z, gate_col0, b_gate, *, bm, bn):
    m, dbr = ya.shape
    n = w_branch.shape[3]
    nj, ni = n // bn, m // bm
    ck = dbr // ni
    assert ck * ni == dbr and ck % BF16_ROWS == 0
    gblk0 = gate_col0 // bn
    row, col, nxt = _ws_maps(nj)
    yspec = pl.BlockSpec((bm, dbr), lambda jj, i: (row(jj, i), 0))

    def wspec(k):
        return pl.BlockSpec((None, None, ck, bn), lambda jj, i: (layer, k, i, nxt(jj)))

    def gspec(k):
        return pl.BlockSpec((bm, bn), lambda jj, i: (row(jj, i), gblk0 + k * nj + col(jj)))

    return pl.pallas_call(
        _branch_body,
        out_shape=_sds((m, n), BF16),
        grid=(nj + 1, ni),
        in_specs=[yspec, yspec, yspec, wspec(0), wspec(1), wspec(2), gspec(0), gspec(1), gspec(2),
                  pl.BlockSpec((3, bn), lambda jj, i: (0, col(jj)))],
        out_specs=pl.BlockSpec((bm, bn), lambda jj, i: (row(jj, i), col(jj))),
        scratch_shapes=[pltpu.VMEM((2, dbr, bn), BF16)] * 3,
        compiler_params=_params("arbitrary", "arbitrary"),
        name="branch_merge",
    )(ya, yb, yc, w_branch, w_branch, w_branch, z, z, z, b_gate)


def _swiglu_body(x_ref, wg_ref, wu_ref, s_ref, o_ref, wgbf_ref, wubf_ref):
    _round_chunk(wg_ref, wgbf_ref)
    _round_chunk(wu_ref, wubf_ref)

    @pl.when(pl.program_id(0) > 0)
    def _():
        x = x_ref[...]
        scale = _row_scale(s_ref, x.shape[1])
        g = jnp.dot(x, _ready(wgbf_ref), preferred_element_type=F32) * scale
        u = jnp.dot(x, _ready(wubf_ref), preferred_element_type=F32) * scale
        o_ref[...] = (g * jax.nn.sigmoid(g) * u).astype(o_ref.dtype)


def _swiglu_up(x, wg, wu, layer, ssq, *, bm, bn):
    m, k = x.shape
    n = wg.shape[2]
    nj, ni = n // bn, m // bm
    ck = k // ni
    assert ck * ni == k and ck % BF16_ROWS == 0
    row, col, nxt = _ws_maps(nj)
    wspec = pl.BlockSpec((None, ck, bn), lambda jj, i: (layer, i, nxt(jj)))
    return pl.pallas_call(
        _swiglu_body,
        out_shape=_sds((m, n), BF16),
        grid=(nj + 1, ni),
        in_specs=[pl.BlockSpec((bm, k), lambda jj, i: (row(jj, i), 0)), wspec, wspec,
                  pl.BlockSpec((ssq.shape[0], bm, LANES), lambda jj, i: (0, row(jj, i), 0))],
        out_specs=pl.BlockSpec((bm, bn), lambda jj, i: (row(jj, i), col(jj))),
        scratch_shapes=[pltpu.VMEM((2, k, bn), BF16), pltpu.VMEM((2, k, bn), BF16)],
        compiler_params=_params("arbitrary", "arbitrary"),
        name="swiglu_up",
    )(x, wg, wu, ssq)


def _halo_rows(k):
    return -(-(k - 1) // SUBLANES) * SUBLANES


def _conv_prompt_body(*refs, taps, rows, gated, shifted):
    if gated:
        h_ref, c_ref, b_ref, w_ref, y_ref, st_ref, ext = refs[:7]
        p = c_ref[...] * h_ref[...]
    else:
        h_ref, c_ref, w_ref, y_ref, st_ref, ext = refs[:6]
        p = h_ref[...] * jax.nn.sigmoid(c_ref[...])
    ph = refs[-1] if shifted else None
    halo = _halo_rows(taps)
    cb = ext.shape[1]

    @pl.when(pl.program_id(2) == 0)
    def _():
        ext[0:halo, :] = jnp.zeros((halo, cb), F32)

    ext[halo:halo + rows, :] = p
    if shifted:
        n = halo + rows - SUBLANES
        for s in range(1, SUBLANES):
            ph[s - 1, 0:n, :] = ext[s:s + n, :]

    def window(off, r0, nr):
        a, s = divmod(off, SUBLANES)
        if s == 0 or not shifted:
            return ext[off + r0:off + r0 + nr, :]
        return ph[s - 1, a * SUBLANES + r0:a * SUBLANES + r0 + nr, :]

    nr = max(SUBLANES, ACC_VREGS * SUBLANES * LANES // cb)
    for r0 in range(0, rows, nr):
        conv = None
        for k in range(taps):
            term = w_ref[k:k + 1, :] * window(halo - (taps - 1) + k, r0, nr)
            conv = term if conv is None else conv + term
        if gated:
            conv = b_ref[r0:r0 + nr, :] * conv
        y_ref[r0:r0 + nr, :] = conv.astype(y_ref.dtype)
    st_ref[...] = ext[halo + rows - (taps - 1):halo + rows, :]
    ext[0:halo, :] = ext[rows:rows + halo, :]


def _conv_prompt(z, col_blocks, w, out_rows, out_dtype, *, n_seq, seq, rows, cb, gated):
    taps, dbr = w.shape
    nt = seq // rows
    ncb = dbr // cb
    halo = _halo_rows(taps)
    shifted = taps - 1 > SUBLANES
    assert rows % max(SUBLANES, ACC_VREGS * SUBLANES * LANES // cb) == 0

    def zspec(col0):
        return pl.BlockSpec((rows, cb), lambda b, c, i: (b * nt + i, col0 // cb + c))

    in_specs = [zspec(c0) for c0 in col_blocks] + [pl.BlockSpec((taps, cb), lambda b, c, i: (0, c))]
    scratch = [pltpu.VMEM((halo + rows, cb), F32)]
    if shifted:
        scratch.append(pltpu.VMEM((SUBLANES - 1, halo + rows, cb), F32))
    y, st = pl.pallas_call(
        functools.partial(_conv_prompt_body, taps=taps, rows=rows, gated=gated, shifted=shifted),
        out_shape=(_sds((out_rows, dbr), out_dtype), _sds((n_seq, taps - 1, dbr), F32)),
        grid=(n_seq, ncb, nt),
        in_specs=in_specs,
        out_specs=(pl.BlockSpec((rows, cb), lambda b, c, i: (b * nt + i, c)),
                   pl.BlockSpec((None, taps - 1, cb), lambda b, c, i: (b, 0, c))),
        scratch_shapes=scratch,
        compiler_params=_params("arbitrary", "arbitrary", "arbitrary"),
        name="conv_prompt",
    )(*([z] * len(col_blocks)), w)
    return y, st


def _conv_sample_body(*refs, taps, steps, nb, gated):
    if gated:
        h_ref, c_ref, b_ref, st_ref, w_ref, ybuf_ref, y_ref, nst_ref, p_ref = refs
        p_ref[...] = c_ref[...] * h_ref[...]
    else:
        h_ref, c_ref, st_ref, w_ref, ybuf_ref, y_ref, nst_ref, p_ref = refs
        p_ref[...] = h_ref[...] * jax.nn.sigmoid(c_ref[...])
    del ybuf_ref

    def slab(t):
        return slice(t * nb, (t + 1) * nb)

    def ext(e):
        return st_ref[slab(e), :] if e < taps - 1 else p_ref[slab(e - (taps - 1)), :]

    for t in range(steps):
        conv = None
        for k in range(taps):
            term = w_ref[k:k + 1, :] * ext(t + k)
            conv = term if conv is None else conv + term
        if gated:
            conv = b_ref[slab(t), :] * conv
        y_ref[slab(t), :] = conv.astype(y_ref.dtype)
    for j in range(taps - 1):
        nst_ref[slab(j), :] = ext(steps + j)


def _conv_sample(z, col_blocks, w, state_tm, ybuf, *, row0, steps, nb, cb, gated):
    taps, dbr = w.shape
    ncb = dbr // cb
    srows = steps * nb
    rblk = row0 // srows

    def zspec(col0):
        return pl.BlockSpec((srows, cb), lambda c: (rblk, col0 // cb + c))

    in_specs = ([zspec(c0) for c0 in col_blocks]
                + [pl.BlockSpec(((taps - 1) * nb, cb), lambda c: (0, c)),
                   pl.BlockSpec((taps, cb), lambda c: (0, c)),
                   pl.BlockSpec(memory_space=pl.ANY)])
    n_in = len(in_specs)
    return pl.pallas_call(
        functools.partial(_conv_sample_body, taps=taps, steps=steps, nb=nb, gated=gated),
        out_shape=(_sds(ybuf.shape, ybuf.dtype), _sds(((taps - 1) * nb, dbr), F32)),
        grid=(ncb,),
        in_specs=in_specs,
        out_specs=(pl.BlockSpec((srows, cb), lambda c: (rblk, c)),
                   pl.BlockSpec(((taps - 1) * nb, cb), lambda c: (0, c))),
        input_output_aliases={n_in - 1: 0},
        scratch_shapes=[pltpu.VMEM((srows, cb), F32)],
        compiler_params=_params("parallel"),
        name="conv_sample",
    )(*([z] * len(col_blocks)), state_tm, w, ybuf)


def _gelu_ln(x, g, b):
    x = jax.nn.gelu(x)
    mu = jnp.mean(x, axis=-1, keepdims=True)
    xc = x - mu
    var = jnp.mean(xc * xc, axis=-1, keepdims=True)
    return xc * lax.rsqrt(var + EPS) * g + b


def _sgu_prompt_body(u_ref, v_ref, ws_ref, bias_ref, g_ref, b_ref, y_ref, *, groups, gd):
    chunk = u_ref.shape[0]
    v = _gelu_ln(v_ref[...], g_ref[...], b_ref[...])
    row = lax.broadcasted_iota(jnp.int32, (chunk, chunk), 0)
    col = lax.broadcasted_iota(jnp.int32, (chunk, chunk), 1)
    causal = col <= row
    for g in range(groups):
        cols = slice(g * gd, (g + 1) * gd)
        wg = jnp.where(causal, ws_ref[g], 0.0).astype(BF16)
        s = jnp.dot(wg, v[:, cols].astype(BF16), preferred_element_type=F32) + bias_ref[:, cols]
        y_ref[:, cols] = (jax.nn.gelu(u_ref[:, cols]) * s).astype(y_ref.dtype)


def _sgu_prompt(z, u_col0, v_col0, w_s, bias_exp, ln_g, ln_b, out_rows, *, n_rows):
    groups, chunk, _ = w_s.shape
    dbr = bias_exp.shape[1]
    vec = pl.BlockSpec((1, dbr), lambda i: (0, 0))
    return pl.pallas_call(
        functools.partial(_sgu_prompt_body, groups=groups, gd=dbr // groups),
        out_shape=_sds((out_rows, dbr), BF16),
        grid=(n_rows // chunk,),
        in_specs=[pl.BlockSpec((chunk, dbr), lambda i: (i, u_col0 // dbr)),
                  pl.BlockSpec((chunk, dbr), lambda i: (i, v_col0 // dbr)),
                  pl.BlockSpec((groups, chunk, chunk), lambda i: (0, 0, 0)),
                  pl.BlockSpec((chunk, dbr), lambda i: (0, 0)), vec, vec],
        out_specs=pl.BlockSpec((chunk, dbr), lambda i: (i, 0)),
        compiler_params=_params("parallel"),
        name="sgu_prompt",
    )(z, z, w_s, bias_exp, ln_g.reshape(1, dbr), ln_b.reshape(1, dbr))


def _sgu_sample_body(u_ref, v_ref, wexp_ref, bexp_ref, g_ref, b_ref, ybuf_ref, y_ref, vout_ref, *, steps, nb):
    del ybuf_ref
    v = _gelu_ln(v_ref[...], g_ref[...], b_ref[...])
    vout_ref[...] = v
    for i in range(steps):
        s = bexp_ref[i:i + 1, :]
        for j in range(i + 1):
            s = s + wexp_ref[i * steps + j:i * steps + j + 1, :] * v[j * nb:(j + 1) * nb, :]
        rows = slice(i * nb, (i + 1) * nb)
        y_ref[rows, :] = (jax.nn.gelu(u_ref[rows, :]) * s).astype(y_ref.dtype)


def _sgu_sample(z, u_col0, v_col0, wexp, bexp, ln_g, ln_b, ybuf, *, row0, steps, nb):
    dbr = bexp.shape[1]
    srows = steps * nb
    rblk = row0 // srows
    vec = pl.BlockSpec((1, dbr), lambda i: (0, 0))
    return pl.pallas_call(
        functools.partial(_sgu_sample_body, steps=steps, nb=nb),
        out_shape=(_sds(ybuf.shape, ybuf.dtype), _sds((srows, dbr), F32)),
        grid=(1,),
        in_specs=[pl.BlockSpec((srows, dbr), lambda i: (rblk, u_col0 // dbr)),
                  pl.BlockSpec((srows, dbr), lambda i: (rblk, v_col0 // dbr)),
                  pl.BlockSpec((steps * steps, dbr), lambda i: (0, 0)),
                  pl.BlockSpec((steps, dbr), lambda i: (0, 0)), vec, vec,
                  pl.BlockSpec(memory_space=pl.ANY)],
        out_specs=(pl.BlockSpec((srows, dbr), lambda i: (rblk, 0)),
                   pl.BlockSpec((srows, dbr), lambda i: (0, 0))),
        input_output_aliases={6: 0},
        compiler_params=_params("arbitrary"),
        name="sgu_sample",
    )(z, z, wexp, bexp, ln_g.reshape(1, dbr), ln_b.reshape(1, dbr), ybuf)


def _attend(q, k, v, scale):
    s = lax.dot_general(q, k, (((1,), (1,)), ((), ())), preferred_element_type=F32) * scale
    e = jnp.exp(s - jnp.max(s, axis=-1, keepdims=True))
    p = e / jnp.sum(e, axis=-1, keepdims=True)
    return jnp.dot(p.astype(BF16), v, preferred_element_type=F32)


def _attn_prompt_body(q_ref, k_ref, v_ref, o_ref, *, scale):
    o = _attend(q_ref[...], k_ref[...].astype(BF16), v_ref[...].astype(BF16), scale)
    o_ref[...] = o.astype(o_ref.dtype)


def _attn_prompt(q, k, v, *, n_seq, seq, n_heads, tq):
    rows, d = q.shape
    hd = d // n_heads
    n_mem = k.shape[0] // n_seq
    nt = seq // tq
    kv = pl.BlockSpec((n_mem, hd), lambda b, h, i: (b, h))
    qo = pl.BlockSpec((tq, hd), lambda b, h, i: (b * nt + i, h))
    return pl.pallas_call(
        functools.partial(_attn_prompt_body, scale=hd ** -0.5),
        out_shape=_sds((rows, d), BF16),
        grid=(n_seq, n_heads, nt),
        in_specs=[qo, kv, kv],
        out_specs=qo,
        compiler_params=_params("parallel", "parallel", "parallel"),
        name="attn_prompt",
    )(q, k, v)


def _attn_sample_body(q_ref, k_hbm, v_hbm, o_ref, kbuf, vbuf, sem, *, layer, scale, hd, steps, n_heads):
    i = pl.program_id(0)

    def fetch(b, slot):
        copies = []
        for h in range(n_heads):
            cols = pl.ds(h * hd, hd)
            copies.append(pltpu.make_async_copy(k_hbm.at[layer, b, :, h, :], kbuf.at[slot, :, cols], sem.at[0, slot]))
            copies.append(pltpu.make_async_copy(v_hbm.at[layer, b, :, h, :], vbuf.at[slot, :, cols], sem.at[1, slot]))
        return copies

    @pl.when(i == 0)
    def _():
        for c in fetch(0, 0):
            c.start()

    @pl.when(i + 1 < pl.num_programs(0))
    def _():
        for c in fetch(i + 1, (i + 1) % 2):
            c.start()

    slot = i % 2
    for c in fetch(i, slot):
        c.wait()
    rows, d = n_heads * steps, q_ref.shape[2]
    row_head = lax.broadcasted_iota(jnp.int32, (rows, d), 0) // steps
    col_head = lax.broadcasted_iota(jnp.int32, (rows, d), 1) // hd
    own = row_head == col_head
    q = jnp.where(own, jnp.concatenate([q_ref[0]] * n_heads, axis=0), 0.0).astype(BF16)
    o = jnp.where(own, _attend(q, kbuf[slot].astype(BF16), vbuf[slot].astype(BF16), scale), 0.0)
    acc = o[0:steps]
    for h in range(1, n_heads):
        acc = acc + o[h * steps:(h + 1) * steps]
    o_ref[0] = acc


def _attn_sample(q, k, v, layer):
    nb, steps, d = q.shape
    n_mem, n_heads, hd = k.shape[2:]
    qo = pl.BlockSpec((1, steps, d), lambda i: (i, 0, 0))
    anyspec = pl.BlockSpec(memory_space=pl.ANY)
    return pl.pallas_call(
        functools.partial(_attn_sample_body, layer=layer, scale=hd ** -0.5, hd=hd, steps=steps, n_heads=n_heads),
        out_shape=_sds((nb, steps, d), F32),
        grid=(nb,),
        in_specs=[qo, anyspec, anyspec],
        out_specs=qo,
        scratch_shapes=[pltpu.VMEM((2, n_mem, d), F32), pltpu.VMEM((2, n_mem, d), F32),
                        pltpu.SemaphoreType.DMA((2, 2))],
        compiler_params=_params("arbitrary"),
        name="attn_sample",
    )(q, k, v)


def kernel(x_prompt, x_sample, cache_mem_k, cache_mem_v, state_conv_a, state_conv_c, mem_prompt, norm_mix, w_in, b_gate, conv_a_w, ln_b_g, ln_b_b, w_s, b_s, conv_c_w, conv_c_b, ln_c_g, ln_c_b, w_branch, w_mix_out, norm_xattn, norm_mem, w_q, w_k, w_v, w_o, norm_ffn, w_gate, w_up, w_down, norm_final):
    n_seq, seq, d = x_prompt.shape
    nb, steps, _ = x_sample.shape
    depth = w_in.shape[0]
    dbr = conv_a_w.shape[-1]
    taps_a, taps_c = conv_a_w.shape[1], conv_c_w.shape[1]
    groups, chunk = w_s.shape[1], w_s.shape[2]
    gd = dbr // groups
    n_mem, n_heads, hd = cache_mem_k.shape[2:]
    mp = n_seq * seq
    ms = nb * steps
    m = mp + ms
    assert steps <= chunk and seq % chunk == 0 and mp % ms == 0 and n_heads * hd == d
    col = {name: i * dbr for i, name in enumerate(("a_h", "a_c", "a_b", "b_u", "b_v", "c_val", "c_gate"))}
    gate_col0 = 7 * dbr

    dff = w_gate.shape[-1]
    bm = _tile(m, 1088, BF16_ROWS)
    bm_half = _tile(m, 544, BF16_ROWS)
    bm_norm = _tile(math.gcd(mp, ms), 512, BF16_ROWS)
    bm_mem = _tile(n_seq * n_mem, 1024, BF16_ROWS)
    bn_d = _tile(d, 1024, LANES)
    bn_mem = _tile(d, 512, LANES)
    bn_in = _tile(w_in.shape[-1], 1024, LANES)
    bn_gate = _tile(math.gcd(d, gate_col0), 512, LANES)
    bn_ff = _tile(dff, 256, LANES)
    bn_down = _tile(d, 256, LANES)
    rows_conv = _tile(seq, 256, SUBLANES)
    cb_a = dbr
    cb_c = _tile(dbr, 512, LANES)
    tq = _tile(seq, 512, BF16_ROWS)

    def tm(a):
        return jnp.swapaxes(a, 0, 1).reshape(a.shape[1] * nb, a.shape[2])

    def bmaj(a, t):
        return jnp.swapaxes(a.reshape(t, nb, a.shape[1]), 0, 1)

    x = jnp.concatenate([x_prompt.reshape(mp, d), tm(x_sample)], axis=0)
    mem = mem_prompt.reshape(n_seq * n_mem, d)
    w_down_bf = _cast_bf16(w_down)

    outs = {k: [] for k in ("mk", "mv", "ca_p", "cc_p", "ca_s", "cc_s", "cv_s")}
    h, ssq = _prenorm(x, norm_mix[0], bm=bm_norm)
    for l in range(depth):
        z = _matmul(h, w_in, l, F32, bm=bm, bn=bn_in, ssq=ssq)

        ya, ca_p = _conv_prompt(z, (col["a_h"], col["a_c"], col["a_b"]), conv_a_w[l], m, BF16,
                                n_seq=n_seq, seq=seq, rows=rows_conv, cb=cb_a, gated=True)
        ya, ca_s = _conv_sample(z, (col["a_h"], col["a_c"], col["a_b"]), conv_a_w[l], tm(state_conv_a[l]), ya,
                                row0=mp, steps=steps, nb=nb, cb=_tile(dbr, 512, LANES), gated=True)

        bias_exp = jnp.repeat(b_s[l].T, gd, axis=1)
        wexp = jnp.repeat(w_s[l][:, :steps, :steps].transpose(1, 2, 0).reshape(steps * steps, groups), gd, axis=1)
        yb = _sgu_prompt(z, col["b_u"], col["b_v"], w_s[l], bias_exp, ln_b_g[l], ln_b_b[l], m, n_rows=mp)
        yb, cv_s = _sgu_sample(z, col["b_u"], col["b_v"], wexp, bias_exp[:steps], ln_b_g[l], ln_b_b[l], yb,
                               row0=mp, steps=steps, nb=nb)

        cc, cc_p = _conv_prompt(z, (col["c_val"], col["c_gate"]), conv_c_w[l], m, F32,
                                n_seq=n_seq, seq=seq, rows=rows_conv, cb=cb_c, gated=False)
        cc, cc_s = _conv_sample(z, (col["c_val"], col["c_gate"]), conv_c_w[l], tm(state_conv_c[l]), cc,
                                row0=mp, steps=steps, nb=nb, cb=_tile(dbr, 256, LANES), gated=False)
        yc = _ln_silu(cc, conv_c_b[l], ln_c_g[l], ln_c_b[l], bm=bm_norm)

        merged = _branch_merge(ya, yb, yc, w_branch, l, z, gate_col0, b_gate[l], bm=bm_half, bn=bn_gate)
        x, hq, ssq_q = _matmul_residual(merged, w_mix_out, l, x, norm_xattn[l], bm=bm_half, bn=bn_d)

        hm = _rmsnorm(mem, norm_mem[l], BF16, bm=_tile(mem.shape[0], 256, BF16_ROWS))
        mk = _matmul(hm, w_k, l, F32, bm=bm_mem, bn=bn_mem)
        mv = _matmul(hm, w_v, l, F32, bm=bm_mem, bn=bn_mem)
        q = _matmul(hq, w_q, l, BF16, bm=bm, bn=bn_d, ssq=ssq_q)
        o = _attn_prompt(q, mk, mv, n_seq=n_seq, seq=seq, n_heads=n_heads, tq=tq)
        o_s = _attn_sample(bmaj(q[mp:], steps).astype(F32), cache_mem_k, cache_mem_v, l)
        o = lax.dynamic_update_slice(o, tm(o_s).astype(BF16), (mp, 0))
        x, hf, ssq_f = _matmul_residual(o, w_o, l, x, norm_ffn[l], bm=bm_half, bn=bn_d)

        act = _swiglu_up(hf, w_gate, w_up, l, ssq_f, bm=bm, bn=bn_ff)
        if l + 1 < depth:
            x, h, ssq = _matmul_residual_rows(act, w_down_bf, l, x, norm_mix[l + 1], bm=bm_half, bn=bn_down)
        else:
            x = _matmul_residual_rows(act, w_down_bf, l, x, None, bm=bm_half, bn=bn_down)

        outs["mk"].append(mk.reshape(n_seq, n_mem, n_heads, hd))
        outs["mv"].append(mv.reshape(n_seq, n_mem, n_heads, hd))
        outs["ca_p"].append(ca_p)
        outs["cc_p"].append(cc_p)
        outs["ca_s"].append(bmaj(ca_s, taps_a - 1))
        outs["cc_s"].append(bmaj(cc_s, taps_c - 1))
        outs["cv_s"].append(bmaj(cv_s, steps))

    y_prompt = _rmsnorm(x, norm_final, F32, bm=bm_norm, n_blocks=mp // bm_norm).reshape(n_seq, seq, d)
    y_sample = bmaj(_rmsnorm(x, norm_final, F32, bm=bm_norm, row_block0=mp // bm_norm, n_blocks=ms // bm_norm), steps)
    return (y_prompt, y_sample, jnp.stack(outs["mk"]), jnp.stack(outs["mv"]), jnp.stack(outs["ca_p"]),
            jnp.stack(outs["cc_p"]), jnp.stack(outs["ca_s"]), jnp.stack(outs["cc_s"]), jnp.stack(outs["cv_s"]))
```

```python
import functools
import math

import jax
import jax.numpy as jnp
from jax import lax
from jax.experimental import pallas as pl
from jax.experimental.pallas import tpu as pltpu

EPS = 1e-6
F32 = jnp.float32
BF16 = jnp.bfloat16
V7X_VMEM_BYTES = 64 * 2**20
VMEM_LIMIT_BYTES = V7X_VMEM_BYTES - 8 * 2**20
SUBLANES = 8
LANES = 128
BF16_ROWS = 16
ACC_VREGS = 16


def _tile(n, target, mult):
    best = None
    for t in range(mult, min(n, target) + 1, mult):
        if n % t == 0:
            best = t
    assert best is not None, (n, target, mult)
    return best


def _params(*sem):
    return pltpu.CompilerParams(dimension_semantics=sem, vmem_limit_bytes=VMEM_LIMIT_BYTES)


def _sds(shape, dtype):
    return jax.ShapeDtypeStruct(shape, dtype)


def _rms_body(x_ref, g_ref, o_ref):
    x = x_ref[...]
    y = x * lax.rsqrt(jnp.mean(x * x, axis=-1, keepdims=True) + EPS)
    o_ref[...] = (y * g_ref[...]).astype(o_ref.dtype)


def _rmsnorm(x, g, out_dtype, *, bm, row_block0=0, n_blocks=None):
    m, d = x.shape
    n_blocks = m // bm if n_blocks is None else n_blocks
    return pl.pallas_call(
        _rms_body,
        out_shape=_sds((n_blocks * bm, d), out_dtype),
        grid=(n_blocks,),
        in_specs=[pl.BlockSpec((bm, d), lambda i: (i + row_block0, 0)),
                  pl.BlockSpec((1, d), lambda i: (0, 0))],
        out_specs=pl.BlockSpec((bm, d), lambda i: (i, 0)),
        compiler_params=_params("parallel"),
        name="rmsnorm",
    )(x, g.reshape(1, d))


def _lane_sumsq(x):
    sq = x * x
    acc = sq[:, 0:LANES]
    for k in range(1, x.shape[1] // LANES):
        acc = acc + sq[:, k * LANES:(k + 1) * LANES]
    return acc


def _row_scale(s_ref, d):
    tot = s_ref[0]
    for p in range(1, s_ref.shape[0]):
        tot = tot + s_ref[p]
    return lax.rsqrt(jnp.sum(tot, axis=-1, keepdims=True) / d + EPS)


def _prenorm_body(xa_ref, xb_ref, g_ref, x_ref, h_ref, s_ref, *, na):
    x = jnp.where(pl.program_id(0) < na, xa_ref[...], xb_ref[...])
    x_ref[...] = x
    h_ref[...] = (x * g_ref[...]).astype(h_ref.dtype)
    s_ref[0] = _lane_sumsq(x)


def _prenorm(xa, xb, g, *, bm):
    (ma, d), mb = xa.shape, xb.shape[0]
    na, m = ma // bm, ma + mb
    row = pl.BlockSpec((bm, d), lambda i: (i, 0))
    return pl.pallas_call(
        functools.partial(_prenorm_body, na=na),
        out_shape=(_sds((m, d), F32), _sds((m, d), BF16), _sds((1, m, LANES), F32)),
        grid=(m // bm,),
        in_specs=[pl.BlockSpec((bm, d), lambda i: (jnp.minimum(i, na - 1), 0)),
                  pl.BlockSpec((bm, d), lambda i: (jnp.maximum(i - na, 0), 0)),
                  pl.BlockSpec((1, d), lambda i: (0, 0))],
        out_specs=(row, row, pl.BlockSpec((1, bm, LANES), lambda i: (0, i, 0))),
        compiler_params=_params("parallel"),
        name="prenorm",
    )(xa, xb, g.reshape(1, d))


def _ln_silu_body(c_ref, cb_ref, g_ref, b_ref, o_ref):
    c = c_ref[...] + cb_ref[...]
    mu = jnp.mean(c, axis=-1, keepdims=True)
    cc = c - mu
    var = jnp.mean(cc * cc, axis=-1, keepdims=True)
    y = cc * lax.rsqrt(var + EPS) * g_ref[...] + b_ref[...]
    o_ref[...] = (y * jax.nn.sigmoid(y)).astype(o_ref.dtype)


def _ln_silu(c, conv_bias, g, b, *, bm):
    m, d = c.shape
    vec = pl.BlockSpec((1, d), lambda i: (0, 0))
    return pl.pallas_call(
        _ln_silu_body,
        out_shape=_sds((m, d), BF16),
        grid=(m // bm,),
        in_specs=[pl.BlockSpec((bm, d), lambda i: (i, 0)), vec, vec, vec],
        out_specs=pl.BlockSpec((bm, d), lambda i: (i, 0)),
        compiler_params=_params("parallel"),
        name="ln_silu",
    )(c, conv_bias.reshape(1, d), g.reshape(1, d), b.reshape(1, d))


def _ws_maps(nj):
    row = lambda jj, i: jnp.where(jj > 0, i, 0)
    col = lambda jj: jnp.maximum(jj - 1, 0)
    nxt = lambda jj: jnp.minimum(jj, nj - 1)
    return row, col, nxt


def _round_chunk(w_ref, wbf_ref):
    ck = w_ref.shape[0]
    r0 = pl.multiple_of(pl.program_id(1) * ck, ck)
    wbf_ref[pl.program_id(0) % 2, pl.ds(r0, ck), :] = w_ref[...].astype(BF16)


def _ready(wbf_ref):
    return wbf_ref[(pl.program_id(0) + 1) % 2]


def _mm_body(*refs, scaled):
    if scaled:
        x_ref, w_ref, s_ref, o_ref, wbf_ref = refs
    else:
        x_ref, w_ref, o_ref, wbf_ref = refs
    _round_chunk(w_ref, wbf_ref)

    @pl.when(pl.program_id(0) > 0)
    def _():
        y = jnp.dot(x_ref[...], _ready(wbf_ref), preferred_element_type=F32)
        if scaled:
            y = y * _row_scale(s_ref, x_ref.shape[1])
        o_ref[...] = y.astype(o_ref.dtype)


def _matmul(x, w, layer, out_dtype, *, bm, bn, ssq=None):
    m, k = x.shape
    n = w.shape[2]
    nj, ni = n // bn, m // bm
    ck = k // ni
    assert ck * ni == k and ck % BF16_ROWS == 0
    row, col, nxt = _ws_maps(nj)
    in_specs = [pl.BlockSpec((bm, k), lambda jj, i: (row(jj, i), 0)),
                pl.BlockSpec((None, ck, bn), lambda jj, i: (layer, i, nxt(jj)))]
    args = [x, w]
    if ssq is not None:
        in_specs.append(pl.BlockSpec((ssq.shape[0], bm, LANES), lambda jj, i: (0, row(jj, i), 0)))
        args.append(ssq)
    return pl.pallas_call(
        functools.partial(_mm_body, scaled=ssq is not None),
        out_shape=_sds((m, n), out_dtype),
        grid=(nj + 1, ni),
        in_specs=in_specs,
        out_specs=pl.BlockSpec((bm, bn), lambda jj, i: (row(jj, i), col(jj))),
        scratch_shapes=[pltpu.VMEM((2, k, bn), BF16)],
        compiler_params=_params("arbitrary", "arbitrary"),
        name="matmul",
    )(*args)


def _mm_res_body(x_ref, w_ref, r_ref, g_ref, o_ref, h_ref, s_ref, wbf_ref):
    _round_chunk(w_ref, wbf_ref)

    @pl.when(pl.program_id(0) > 0)
    def _():
        y = r_ref[...] + jnp.dot(x_ref[...], _ready(wbf_ref), preferred_element_type=F32)
        o_ref[...] = y
        h_ref[...] = (y * g_ref[...]).astype(h_ref.dtype)
        s_ref[...] = _lane_sumsq(y)


def _matmul_residual(x, w, layer, res, g_next, *, bm, bn):
    m, k = x.shape
    n = w.shape[2]
    nj, ni = n // bn, m // bm
    ck = k // ni
    assert ck * ni == k and ck % BF16_ROWS == 0
    row, col, nxt = _ws_maps(nj)
    ij = lambda jj, i: (row(jj, i), col(jj))
    return pl.pallas_call(
        _mm_res_body,
        out_shape=(_sds((m, n), F32), _sds((m, n), BF16), _sds((nj, m, LANES), F32)),
        grid=(nj + 1, ni),
        in_specs=[pl.BlockSpec((bm, k), lambda jj, i: (row(jj, i), 0)),
                  pl.BlockSpec((None, ck, bn), lambda jj, i: (layer, i, nxt(jj))),
                  pl.BlockSpec((bm, bn), ij),
                  pl.BlockSpec((1, bn), lambda jj, i: (0, col(jj)))],
        out_specs=(pl.BlockSpec((bm, bn), ij), pl.BlockSpec((bm, bn), ij),
                   pl.BlockSpec((None, bm, LANES), lambda jj, i: (col(jj), row(jj, i), 0))),
        scratch_shapes=[pltpu.VMEM((2, k, bn), BF16)],
        input_output_aliases={2: 0},
        compiler_params=_params("arbitrary", "arbitrary"),
        name="matmul_residual",
    )(x, w, res, g_next.reshape(1, n))


def _mm_res_rows_body(*refs, with_norm):
    if with_norm:
        x_ref, w_ref, r_ref, g_ref, o_ref, h_ref, s_ref = refs
    else:
        x_ref, w_ref, r_ref, o_ref = refs
    y = r_ref[...] + jnp.dot(x_ref[...], w_ref[...], preferred_element_type=F32)
    o_ref[...] = y
    if with_norm:
        h_ref[...] = (y * g_ref[...]).astype(h_ref.dtype)
        part = _lane_sumsq(y)

        @pl.when(pl.program_id(1) == 0)
        def _():
            s_ref[0] = part

        @pl.when(pl.program_id(1) > 0)
        def _():
            s_ref[0] = s_ref[0] + part


def _matmul_residual_rows(x, w, res, g_next, *, bm, bn):
    m, k = x.shape
    n = w.shape[1]
    ij = lambda i, j: (i, j)
    with_norm = g_next is not None
    in_specs = [pl.BlockSpec((bm, k), lambda i, j: (i, 0)),
                pl.BlockSpec((k, bn), lambda i, j: (0, j)),
                pl.BlockSpec((bm, bn), ij)]
    out_shape = [_sds((m, n), F32)]
    out_specs = [pl.BlockSpec((bm, bn), ij)]
    args = [x, w, res]
    if with_norm:
        in_specs.append(pl.BlockSpec((1, bn), lambda i, j: (0, j)))
        args.append(g_next.reshape(1, n))
        out_shape += [_sds((m, n), BF16), _sds((1, m, LANES), F32)]
        out_specs += [pl.BlockSpec((bm, bn), ij), pl.BlockSpec((1, bm, LANES), lambda i, j: (0, i, 0))]
    out = pl.pallas_call(
        functools.partial(_mm_res_rows_body, with_norm=with_norm),
        out_shape=tuple(out_shape),
        grid=(m // bm, n // bn),
        in_specs=in_specs,
        out_specs=tuple(out_specs),
        input_output_aliases={2: 0},
        compiler_params=_params("arbitrary", "arbitrary"),
        name="matmul_residual_rows",
    )(*args)
    return out if with_norm else out[0]


def _branch_body(ya_ref, yb_ref, yc_ref, wa_ref, wb_ref, wc_ref, ga_ref, gb_ref, gc_ref, bg_ref, o_ref,
                 wabf_ref, wbbf_ref, wcbf_ref):
    branches = ((ya_ref, wa_ref, ga_ref, wabf_ref), (yb_ref, wb_ref, gb_ref, wbbf_ref), (yc_ref, wc_ref, gc_ref, wcbf_ref))
    for _, w_ref, _, wbf_ref in branches:
        _round_chunk(w_ref, wbf_ref)

    @pl.when(pl.program_id(0) > 0)
    def _():
        acc = None
        for k, (y_ref, _, g_ref, wbf_ref) in enumerate(branches):
            d = jnp.dot(y_ref[...], _ready(wbf_ref), preferred_element_type=F32)
            t = jax.nn.sigmoid(g_ref[...] + bg_ref[k:k + 1, :]) * d
            acc = t if acc is None else acc + t
        o_ref[...] = acc.astype(o_ref.dtype)


def _branch_merge(ya, yb, yc, w_branch, layer, z, gate_col0, b_gate, *, bm, bn):
    m, dbr = ya.shape
    n = w_branch.shape[3]
    nj, ni = n // bn, m // bm
    ck = dbr // ni
    assert ck * ni == dbr and ck % BF16_ROWS == 0
    gblk0 = gate_col0 // bn
    row, col, nxt = _ws_maps(nj)
    yspec = pl.BlockSpec((bm, dbr), lambda jj, i: (row(jj, i), 0))

    def wspec(k):
        return pl.BlockSpec((None, None, ck, bn), lambda jj, i: (layer, k, i, nxt(jj)))

    def gspec(k):
        return pl.BlockSpec((bm, bn), lambda jj, i: (row(jj, i), gblk0 + k * nj + col(jj)))

    return pl.pallas_call(
        _branch_body,
        out_shape=_sds((m, n), BF16),
        grid=(nj + 1, ni),
        in_specs=[yspec, yspec, yspec, wspec(0), wspec(1), wspec(2), gspec(0), gspec(1), gspec(2),
                  pl.BlockSpec((3, bn), lambda jj, i: (0, col(jj)))],
        out_specs=pl.BlockSpec((bm, bn), lambda jj, i: (row(jj, i), col(jj))),
        scratch_shapes=[pltpu.VMEM((2, dbr, bn), BF16)] * 3,
        compiler_params=_params("arbitrary", "arbitrary"),
        name="branch_merge",
    )(ya, yb, yc, w_branch, w_branch, w_branch, z, z, z, b_gate)


def _swiglu_body(x_ref, wg_ref, wu_ref, s_ref, wd_ref, o_ref, wdbf_ref, wgbf_ref, wubf_ref):
    _round_chunk(wg_ref, wgbf_ref)
    _round_chunk(wu_ref, wubf_ref)

    @pl.when(pl.program_id(0) > 0)
    def _():
        wdbf_ref[...] = wd_ref[...].astype(wdbf_ref.dtype)
        x = x_ref[...]
        scale = _row_scale(s_ref, x.shape[1])
        g = jnp.dot(x, _ready(wgbf_ref), preferred_element_type=F32) * scale
        u = jnp.dot(x, _ready(wubf_ref), preferred_element_type=F32) * scale
        o_ref[...] = (g * jax.nn.sigmoid(g) * u).astype(o_ref.dtype)


def _swiglu_up(x, wg, wu, w_down, layer, ssq, *, bm, bn):
    m, k = x.shape
    n = wg.shape[2]
    kd, nd = w_down.shape[1:]
    nj, ni = n // bn, m // bm
    ck = k // ni
    rd = kd // (nj * ni)
    assert ck * ni == k and ck % BF16_ROWS == 0 and rd * nj * ni == kd and rd % BF16_ROWS == 0
    row, col, nxt = _ws_maps(nj)
    slab = lambda jj, i: jnp.where(jj > 0, (jj - 1) * ni + i, 0)
    wspec = pl.BlockSpec((None, ck, bn), lambda jj, i: (layer, i, nxt(jj)))
    return pl.pallas_call(
        _swiglu_body,
        out_shape=(_sds((m, n), BF16), _sds((kd, nd), BF16)),
        grid=(nj + 1, ni),
        in_specs=[pl.BlockSpec((bm, k), lambda jj, i: (row(jj, i), 0)), wspec, wspec,
                  pl.BlockSpec((ssq.shape[0], bm, LANES), lambda jj, i: (0, row(jj, i), 0)),
                  pl.BlockSpec((None, rd, nd), lambda jj, i: (layer, slab(jj, i), 0))],
        out_specs=(pl.BlockSpec((bm, bn), lambda jj, i: (row(jj, i), col(jj))),
                   pl.BlockSpec((rd, nd), lambda jj, i: (slab(jj, i), 0))),
        scratch_shapes=[pltpu.VMEM((2, k, bn), BF16), pltpu.VMEM((2, k, bn), BF16)],
        compiler_params=_params("arbitrary", "arbitrary"),
        name="swiglu_up",
    )(x, wg, wu, ssq, w_down)


def _halo_rows(k):
    return -(-(k - 1) // SUBLANES) * SUBLANES


def _conv_prompt_body(*refs, taps, rows, gated, shifted):
    if gated:
        h_ref, c_ref, b_ref, w_ref, y_ref, st_ref, ext = refs[:7]
        p = c_ref[...] * h_ref[...]
    else:
        h_ref, c_ref, w_ref, y_ref, st_ref, ext = refs[:6]
        p = h_ref[...] * jax.nn.sigmoid(c_ref[...])
    ph = refs[-1] if shifted else None
    halo = _halo_rows(taps)
    cb = ext.shape[1]

    @pl.when(pl.program_id(2) == 0)
    def _():
        ext[0:halo, :] = jnp.zeros((halo, cb), F32)

    ext[halo:halo + rows, :] = p
    if shifted:
        n = halo + rows - SUBLANES
        for s in range(1, SUBLANES):
            ph[s - 1, 0:n, :] = ext[s:s + n, :]

    def window(off, r0, nr):
        a, s = divmod(off, SUBLANES)
        if s == 0 or not shifted:
            return ext[off + r0:off + r0 + nr, :]
        return ph[s - 1, a * SUBLANES + r0:a * SUBLANES + r0 + nr, :]

    nr = max(SUBLANES, ACC_VREGS * SUBLANES * LANES // cb)
    for r0 in range(0, rows, nr):
        conv = None
        for k in range(taps):
            term = w_ref[k:k + 1, :] * window(halo - (taps - 1) + k, r0, nr)
            conv = term if conv is None else conv + term
        if gated:
            conv = b_ref[r0:r0 + nr, :] * conv
        y_ref[r0:r0 + nr, :] = conv.astype(y_ref.dtype)
    st_ref[...] = ext[halo + rows - (taps - 1):halo + rows, :]
    ext[0:halo, :] = ext[rows:rows + halo, :]


def _conv_prompt(z, col_blocks, w, out_rows, out_dtype, *, n_seq, seq, rows, cb, gated):
    taps, dbr = w.shape
    nt = seq // rows
    ncb = dbr // cb
    halo = _halo_rows(taps)
    shifted = taps - 1 > SUBLANES
    assert rows % max(SUBLANES, ACC_VREGS * SUBLANES * LANES // cb) == 0

    def zspec(col0):
        return pl.BlockSpec((rows, cb), lambda b, c, i: (b * nt + i, col0 // cb + c))

    in_specs = [zspec(c0) for c0 in col_blocks] + [pl.BlockSpec((taps, cb), lambda b, c, i: (0, c))]
    scratch = [pltpu.VMEM((halo + rows, cb), F32)]
    if shifted:
        scratch.append(pltpu.VMEM((SUBLANES - 1, halo + rows, cb), F32))
    y, st = pl.pallas_call(
        functools.partial(_conv_prompt_body, taps=taps, rows=rows, gated=gated, shifted=shifted),
        out_shape=(_sds((out_rows, dbr), out_dtype), _sds((n_seq, taps - 1, dbr), F32)),
        grid=(n_seq, ncb, nt),
        in_specs=in_specs,
        out_specs=(pl.BlockSpec((rows, cb), lambda b, c, i: (b * nt + i, c)),
                   pl.BlockSpec((None, taps - 1, cb), lambda b, c, i: (b, 0, c))),
        scratch_shapes=scratch,
        compiler_params=_params("arbitrary", "arbitrary", "arbitrary"),
        name="conv_prompt",
    )(*([z] * len(col_blocks)), w)
    return y, st


def _conv_sample_body(*refs, taps, steps, nb, gated):
    if gated:
        h_ref, c_ref, b_ref, st_ref, w_ref, ybuf_ref, y_ref, nst_ref, p_ref = refs
        p_ref[...] = c_ref[...] * h_ref[...]
    else:
        h_ref, c_ref, st_ref, w_ref, ybuf_ref, y_ref, nst_ref, p_ref = refs
        p_ref[...] = h_ref[...] * jax.nn.sigmoid(c_ref[...])
    del ybuf_ref

    def slab(t):
        return slice(t * nb, (t + 1) * nb)

    def ext(e):
        return st_ref[slab(e), :] if e < taps - 1 else p_ref[slab(e - (taps - 1)), :]

    for t in range(steps):
        conv = None
        for k in range(taps):
            term = w_ref[k:k + 1, :] * ext(t + k)
            conv = term if conv is None else conv + term
        if gated:
            conv = b_ref[slab(t), :] * conv
        y_ref[slab(t), :] = conv.astype(y_ref.dtype)
    for j in range(taps - 1):
        nst_ref[slab(j), :] = ext(steps + j)


def _conv_sample(z, col_blocks, w, state_tm, ybuf, *, row0, steps, nb, cb, gated):
    taps, dbr = w.shape
    ncb = dbr // cb
    srows = steps * nb
    rblk = row0 // srows

    def zspec(col0):
        return pl.BlockSpec((srows, cb), lambda c: (rblk, col0 // cb + c))

    in_specs = ([zspec(c0) for c0 in col_blocks]
                + [pl.BlockSpec(((taps - 1) * nb, cb), lambda c: (0, c)),
                   pl.BlockSpec((taps, cb), lambda c: (0, c)),
                   pl.BlockSpec(memory_space=pl.ANY)])
    n_in = len(in_specs)
    return pl.pallas_call(
        functools.partial(_conv_sample_body, taps=taps, steps=steps, nb=nb, gated=gated),
        out_shape=(_sds(ybuf.shape, ybuf.dtype), _sds(((taps - 1) * nb, dbr), F32)),
        grid=(ncb,),
        in_specs=in_specs,
        out_specs=(pl.BlockSpec((srows, cb), lambda c: (rblk, c)),
                   pl.BlockSpec(((taps - 1) * nb, cb), lambda c: (0, c))),
        input_output_aliases={n_in - 1: 0},
        scratch_shapes=[pltpu.VMEM((srows, cb), F32)],
        compiler_params=_params("parallel"),
        name="conv_sample",
    )(*([z] * len(col_blocks)), state_tm, w, ybuf)


def _gelu_ln(x, g, b):
    x = jax.nn.gelu(x)
    mu = jnp.mean(x, axis=-1, keepdims=True)
    xc = x - mu
    var = jnp.mean(xc * xc, axis=-1, keepdims=True)
    return xc * lax.rsqrt(var + EPS) * g + b


def _sgu_prompt_body(u_ref, v_ref, ws_ref, bias_ref, g_ref, b_ref, y_ref, *, groups, gd):
    chunk = u_ref.shape[0]
    v = _gelu_ln(v_ref[...], g_ref[...], b_ref[...])
    row = lax.broadcasted_iota(jnp.int32, (chunk, chunk), 0)
    col = lax.broadcasted_iota(jnp.int32, (chunk, chunk), 1)
    causal = col <= row
    for g in range(groups):
        cols = slice(g * gd, (g + 1) * gd)
        wg = jnp.where(causal, ws_ref[g], 0.0).astype(BF16)
        s = jnp.dot(wg, v[:, cols].astype(BF16), preferred_element_type=F32) + bias_ref[:, cols]
        y_ref[:, cols] = (jax.nn.gelu(u_ref[:, cols]) * s).astype(y_ref.dtype)


def _sgu_prompt(z, u_col0, v_col0, w_s, bias_exp, ln_g, ln_b, out_rows, *, n_rows):
    groups, chunk, _ = w_s.shape
    dbr = bias_exp.shape[1]
    vec = pl.BlockSpec((1, dbr), lambda i: (0, 0))
    return pl.pallas_call(
        functools.partial(_sgu_prompt_body, groups=groups, gd=dbr // groups),
        out_shape=_sds((out_rows, dbr), BF16),
        grid=(n_rows // chunk,),
        in_specs=[pl.BlockSpec((chunk, dbr), lambda i: (i, u_col0 // dbr)),
                  pl.BlockSpec((chunk, dbr), lambda i: (i, v_col0 // dbr)),
                  pl.BlockSpec((groups, chunk, chunk), lambda i: (0, 0, 0)),
                  pl.BlockSpec((chunk, dbr), lambda i: (0, 0)), vec, vec],
        out_specs=pl.BlockSpec((chunk, dbr), lambda i: (i, 0)),
        compiler_params=_params("parallel"),
        name="sgu_prompt",
    )(z, z, w_s, bias_exp, ln_g.reshape(1, dbr), ln_b.reshape(1, dbr))


def _sgu_sample_body(u_ref, v_ref, wexp_ref, bexp_ref, g_ref, b_ref, ybuf_ref, y_ref, vout_ref, *, steps, nb):
    del ybuf_ref
    v = _gelu_ln(v_ref[...], g_ref[...], b_ref[...])
    vout_ref[...] = v
    for i in range(steps):
        s = bexp_ref[i:i + 1, :]
        for j in range(i + 1):
            s = s + wexp_ref[i * steps + j:i * steps + j + 1, :] * v[j * nb:(j + 1) * nb, :]
        rows = slice(i * nb, (i + 1) * nb)
        y_ref[rows, :] = (jax.nn.gelu(u_ref[rows, :]) * s).astype(y_ref.dtype)


def _sgu_sample(z, u_col0, v_col0, wexp, bexp, ln_g, ln_b, ybuf, *, row0, steps, nb):
    dbr = bexp.shape[1]
    srows = steps * nb
    rblk = row0 // srows
    vec = pl.BlockSpec((1, dbr), lambda i: (0, 0))
    return pl.pallas_call(
        functools.partial(_sgu_sample_body, steps=steps, nb=nb),
        out_shape=(_sds(ybuf.shape, ybuf.dtype), _sds((srows, dbr), F32)),
        grid=(1,),
        in_specs=[pl.BlockSpec((srows, dbr), lambda i: (rblk, u_col0 // dbr)),
                  pl.BlockSpec((srows, dbr), lambda i: (rblk, v_col0 // dbr)),
                  pl.BlockSpec((steps * steps, dbr), lambda i: (0, 0)),
                  pl.BlockSpec((steps, dbr), lambda i: (0, 0)), vec, vec,
                  pl.BlockSpec(memory_space=pl.ANY)],
        out_specs=(pl.BlockSpec((srows, dbr), lambda i: (rblk, 0)),
                   pl.BlockSpec((srows, dbr), lambda i: (0, 0))),
        input_output_aliases={6: 0},
        compiler_params=_params("arbitrary"),
        name="sgu_sample",
    )(z, z, wexp, bexp, ln_g.reshape(1, dbr), ln_b.reshape(1, dbr), ybuf)


def _attend(q, k, v, scale):
    s = lax.dot_general(q, k, (((1,), (1,)), ((), ())), preferred_element_type=F32) * scale
    e = jnp.exp(s - jnp.max(s, axis=-1, keepdims=True))
    p = e / jnp.sum(e, axis=-1, keepdims=True)
    return jnp.dot(p.astype(BF16), v, preferred_element_type=F32)


def _attn_prompt_body(q_ref, k_ref, v_ref, o_ref, *, scale):
    o = _attend(q_ref[...], k_ref[...].astype(BF16), v_ref[...].astype(BF16), scale)
    o_ref[...] = o.astype(o_ref.dtype)


def _attn_prompt(q, k, v, *, n_seq, seq, n_heads, tq):
    rows, d = q.shape
    hd = d // n_heads
    n_mem = k.shape[0] // n_seq
    nt = seq // tq
    kv = pl.BlockSpec((n_mem, hd), lambda b, h, i: (b, h))
    qo = pl.BlockSpec((tq, hd), lambda b, h, i: (b * nt + i, h))
    return pl.pallas_call(
        functools.partial(_attn_prompt_body, scale=hd ** -0.5),
        out_shape=_sds((rows, d), BF16),
        grid=(n_seq, n_heads, nt),
        in_specs=[qo, kv, kv],
        out_specs=qo,
        compiler_params=_params("parallel", "parallel", "parallel"),
        name="attn_prompt",
    )(q, k, v)


def _attn_sample_body(q_ref, k_hbm, v_hbm, o_ref, kbuf, vbuf, sem, *, layer, scale, hd, steps, n_heads):
    i = pl.program_id(0)

    def fetch(b, slot):
        copies = []
        for h in range(n_heads):
            cols = pl.ds(h * hd, hd)
            copies.append(pltpu.make_async_copy(k_hbm.at[layer, b, :, h, :], kbuf.at[slot, :, cols], sem.at[0, slot]))
            copies.append(pltpu.make_async_copy(v_hbm.at[layer, b, :, h, :], vbuf.at[slot, :, cols], sem.at[1, slot]))
        return copies

    @pl.when(i == 0)
    def _():
        for c in fetch(0, 0):
            c.start()

    @pl.when(i + 1 < pl.num_programs(0))
    def _():
        for c in fetch(i + 1, (i + 1) % 2):
            c.start()

    slot = i % 2
    for c in fetch(i, slot):
        c.wait()
    rows, d = n_heads * steps, q_ref.shape[2]
    row_head = lax.broadcasted_iota(jnp.int32, (rows, d), 0) // steps
    col_head = lax.broadcasted_iota(jnp.int32, (rows, d), 1) // hd
    own = row_head == col_head
    q = jnp.where(own, jnp.concatenate([q_ref[0]] * n_heads, axis=0), 0.0).astype(BF16)
    o = jnp.where(own, _attend(q, kbuf[slot].astype(BF16), vbuf[slot].astype(BF16), scale), 0.0)
    acc = o[0:steps]
    for h in range(1, n_heads):
        acc = acc + o[h * steps:(h + 1) * steps]
    o_ref[0] = acc


def _attn_sample(q, k, v, layer):
    nb, steps, d = q.shape
    n_mem, n_heads, hd = k.shape[2:]
    qo = pl.BlockSpec((1, steps, d), lambda i: (i, 0, 0))
    anyspec = pl.BlockSpec(memory_space=pl.ANY)
    return pl.pallas_call(
        functools.partial(_attn_sample_body, layer=layer, scale=hd ** -0.5, hd=hd, steps=steps, n_heads=n_heads),
        out_shape=_sds((nb, steps, d), F32),
        grid=(nb,),
        in_specs=[qo, anyspec, anyspec],
        out_specs=qo,
        scratch_shapes=[pltpu.VMEM((2, n_mem, d), F32), pltpu.VMEM((2, n_mem, d), F32),
                        pltpu.SemaphoreType.DMA((2, 2))],
        compiler_params=_params("arbitrary"),
        name="attn_sample",
    )(q, k, v)


def kernel(x_prompt, x_sample, cache_mem_k, cache_mem_v, state_conv_a, state_conv_c, mem_prompt, norm_mix, w_in, b_gate, conv_a_w, ln_b_g, ln_b_b, w_s, b_s, conv_c_w, conv_c_b, ln_c_g, ln_c_b, w_branch, w_mix_out, norm_xattn, norm_mem, w_q, w_k, w_v, w_o, norm_ffn, w_gate, w_up, w_down, norm_final):
    n_seq, seq, d = x_prompt.shape
    nb, steps, _ = x_sample.shape
    depth = w_in.shape[0]
    dbr = conv_a_w.shape[-1]
    taps_a, taps_c = conv_a_w.shape[1], conv_c_w.shape[1]
    groups, chunk = w_s.shape[1], w_s.shape[2]
    gd = dbr // groups
    n_mem, n_heads, hd = cache_mem_k.shape[2:]
    mp = n_seq * seq
    ms = nb * steps
    m = mp + ms
    assert steps <= chunk and seq % chunk == 0 and mp % ms == 0 and n_heads * hd == d
    col = {name: i * dbr for i, name in enumerate(("a_h", "a_c", "a_b", "b_u", "b_v", "c_val", "c_gate"))}
    gate_col0 = 7 * dbr

    dff = w_gate.shape[-1]
    bm = _tile(m, 1088, BF16_ROWS)
    bm_half = _tile(m, 544, BF16_ROWS)
    bm_norm = _tile(math.gcd(mp, ms), 512, BF16_ROWS)
    bm_mem = _tile(n_seq * n_mem, 1024, BF16_ROWS)
    bn_d = _tile(d, 1024, LANES)
    bn_mem = _tile(d, 512, LANES)
    bn_in = _tile(w_in.shape[-1], 1024, LANES)
    bn_gate = _tile(math.gcd(d, gate_col0), 512, LANES)
    bn_ff = _tile(dff, 256, LANES)
    bn_down = _tile(d, 256, LANES)
    rows_conv = _tile(seq, 256, SUBLANES)
    cb_a = dbr
    cb_c = _tile(dbr, 512, LANES)
    tq = _tile(seq, 512, BF16_ROWS)

    def tm(a):
        return jnp.swapaxes(a, 0, 1).reshape(a.shape[1] * nb, a.shape[2])

    def bmaj(a, t):
        return jnp.swapaxes(a.reshape(t, nb, a.shape[1]), 0, 1)

    mem = mem_prompt.reshape(n_seq * n_mem, d)

    outs = {k: [] for k in ("mk", "mv", "ca_p", "cc_p", "ca_s", "cc_s", "cv_s")}
    x, h, ssq = _prenorm(x_prompt.reshape(mp, d), tm(x_sample), norm_mix[0], bm=_tile(math.gcd(mp, ms), 256, BF16_ROWS))
    for l in range(depth):
        z = _matmul(h, w_in, l, F32, bm=bm, bn=bn_in, ssq=ssq)

        ya, ca_p = _conv_prompt(z, (col["a_h"], col["a_c"], col["a_b"]), conv_a_w[l], m, BF16,
                                n_seq=n_seq, seq=seq, rows=rows_conv, cb=cb_a, gated=True)
        ya, ca_s = _conv_sample(z, (col["a_h"], col["a_c"], col["a_b"]), conv_a_w[l], tm(state_conv_a[l]), ya,
                                row0=mp, steps=steps, nb=nb, cb=_tile(dbr, 512, LANES), gated=True)

        bias_exp = jnp.repeat(b_s[l].T, gd, axis=1)
        wexp = jnp.repeat(w_s[l][:, :steps, :steps].transpose(1, 2, 0).reshape(steps * steps, groups), gd, axis=1)
        yb = _sgu_prompt(z, col["b_u"], col["b_v"], w_s[l], bias_exp, ln_b_g[l], ln_b_b[l], m, n_rows=mp)
        yb, cv_s = _sgu_sample(z, col["b_u"], col["b_v"], wexp, bias_exp[:steps], ln_b_g[l], ln_b_b[l], yb,
                               row0=mp, steps=steps, nb=nb)

        cc, cc_p = _conv_prompt(z, (col["c_val"], col["c_gate"]), conv_c_w[l], m, F32,
                                n_seq=n_seq, seq=seq, rows=rows_conv, cb=cb_c, gated=False)
        cc, cc_s = _conv_sample(z, (col["c_val"], col["c_gate"]), conv_c_w[l], tm(state_conv_c[l]), cc,
                                row0=mp, steps=steps, nb=nb, cb=_tile(dbr, 256, LANES), gated=False)
        yc = _ln_silu(cc, conv_c_b[l], ln_c_g[l], ln_c_b[l], bm=bm_norm)

        merged = _branch_merge(ya, yb, yc, w_branch, l, z, gate_col0, b_gate[l], bm=bm_half, bn=bn_gate)
        x, hq, ssq_q = _matmul_residual(merged, w_mix_out, l, x, norm_xattn[l], bm=bm_half, bn=bn_d)

        hm = _rmsnorm(mem, norm_mem[l], BF16, bm=_tile(mem.shape[0], 256, BF16_ROWS))
        mk = _matmul(hm, w_k, l, F32, bm=bm_mem, bn=bn_mem)
        mv = _matmul(hm, w_v, l, F32, bm=bm_mem, bn=bn_mem)
        q = _matmul(hq, w_q, l, BF16, bm=bm, bn=bn_d, ssq=ssq_q)
        o = _attn_prompt(q, mk, mv, n_seq=n_seq, seq=seq, n_heads=n_heads, tq=tq)
        o_s = _attn_sample(bmaj(q[mp:], steps).astype(F32), cache_mem_k, cache_mem_v, l)
        o = lax.dynamic_update_slice(o, tm(o_s).astype(BF16), (mp, 0))
        x, hf, ssq_f = _matmul_residual(o, w_o, l, x, norm_ffn[l], bm=bm_half, bn=bn_d)

        act, w_down_bf = _swiglu_up(hf, w_gate, w_up, w_down, l, ssq_f, bm=bm, bn=bn_ff)
        if l + 1 < depth:
            x, h, ssq = _matmul_residual_rows(act, w_down_bf, x, norm_mix[l + 1], bm=bm_half, bn=bn_down)
        else:
            x = _matmul_residual_rows(act, w_down_bf, x, None, bm=bm_half, bn=bn_down)

        outs["mk"].append(mk.reshape(n_seq, n_mem, n_heads, hd))
        outs["mv"].append(mv.reshape(n_seq, n_mem, n_heads, hd))
        outs["ca_p"].append(ca_p)
        outs["cc_p"].append(cc_p)
        outs["ca_s"].append(bmaj(ca_s, taps_a - 1))
        outs["cc_s"].append(bmaj(cc_s, taps_c - 1))
        outs["cv_s"].append(bmaj(cv_s, steps))

    y_prompt = _rmsnorm(x, norm_final, F32, bm=bm_norm, n_blocks=mp // bm_norm).reshape(n_seq, seq, d)
    y_sample = bmaj(_rmsnorm(x, norm_final, F32, bm=bm_norm, row_block0=mp // bm_norm, n_blocks=ms // bm_norm), steps)
    return (y_prompt, y_sample, jnp.stack(outs["mk"]), jnp.stack(outs["mv"]), jnp.stack(outs["ca_p"]),
            jnp.stack(outs["cc_p"]), jnp.stack(outs["ca_s"]), jnp.stack(outs["cc_s"]), jnp.stack(outs["cv_s"]))
```
